```python
import jax
import jax.numpy as jnp
from jax import lax
import numpy as np

D_MODEL = 2048
BATCH = 4
SEQ = 4096
DEPTH = 4

MIX_WIDTH = D_MODEL
EPS = 1e-6
GLA_HEADS = 4
GLA_DV = (MIX_WIDTH // 4) // GLA_HEADS
GLA_DK = GLA_DV // 2
GLA_RANK = 16
GLA_NORMALIZER = 16.0
GLA_CHUNK = 64
SWA_HEADS = 16
SWA_KV_HEADS = 2
SWA_HD = (MIX_WIDTH // 2) // SWA_HEADS
WINDOW = 128
ROPE_THETA = 500000.0
ROPE_DIMS = SWA_HD // 4
GDN_HEADS = 4
GDN_DK = (MIX_WIDTH // 4) // GDN_HEADS
GDN_DV = GDN_DK
CONV_WIDTH = 4
GDN_CHUNK = 64
GDN_CONV_CH = GDN_HEADS * (2 * GDN_DK + GDN_DV)
D_FF = ((8 * D_MODEL + 3 * 256 - 1) // (3 * 256)) * 256

GLA_SIZES = (GLA_HEADS * GLA_DK, GLA_HEADS * GLA_DK, GLA_HEADS * GLA_DV, GLA_HEADS * GLA_DV, GLA_RANK)
SWA_SIZES = (SWA_HEADS * SWA_HD, SWA_KV_HEADS * SWA_HD, SWA_KV_HEADS * SWA_HD)
GDN_SIZES = (GDN_HEADS * GDN_DK, GDN_HEADS * GDN_DK, GDN_HEADS * GDN_DV, GDN_HEADS * GDN_DV, GDN_HEADS, GDN_HEADS)
IN_SIZES = GLA_SIZES + SWA_SIZES + GDN_SIZES
IN_WIDTH = sum(IN_SIZES)

kernel_name = "hybrid_gla_swa_gdn_parallel_heads"


def _split_columns(t, sizes):
    idx, acc = [], 0
    for s in sizes[:-1]:
        acc += s
        idx.append(acc)
    return jnp.split(t, idx, axis=-1)


def rms_norm(x, gain):
    xf = x.astype(jnp.float32)
    y = xf * lax.rsqrt(jnp.mean(xf * xf, axis=-1, keepdims=True) + EPS)
    return (y * gain.astype(jnp.float32)).astype(x.dtype)


def l2_normalize(t):
    return t * lax.rsqrt(jnp.sum(t * t, axis=-1, keepdims=True) + EPS)


def partial_rotary(x, positions):
    half = ROPE_DIMS // 2
    inv_freq = ROPE_THETA ** (-jnp.arange(half, dtype=jnp.float32) / half)
    ang = positions.astype(jnp.float32)[..., None] * inv_freq
    cos = jnp.cos(ang)[:, :, None, :]
    sin = jnp.sin(ang)[:, :, None, :]
    xr = x[..., :ROPE_DIMS].astype(jnp.float32)
    x1, x2 = xr[..., :half], xr[..., half:]
    rot = jnp.concatenate([x1 * cos - x2 * sin, x2 * cos + x1 * sin], axis=-1)
    return jnp.concatenate([rot.astype(x.dtype), x[..., ROPE_DIMS:]], axis=-1)


def _to_chunks(t, chunk):
    B, T, H, d = t.shape
    return t.reshape(B, T // chunk, chunk, H, d).transpose(1, 0, 3, 2, 4)


def _from_chunks(t):
    N, B, H, C, d = t.shape
    return t.transpose(1, 0, 3, 2, 4).reshape(B, N * C, H, d)


def gla_chunked(q, k, v, log_a):
    B, T, H, dk = q.shape
    dv = v.shape[-1]
    C = GLA_CHUNK
    qc = _to_chunks(q * dk ** -0.5, C)
    kc = _to_chunks(k, C)
    vc = _to_chunks(v, C)
    bc = jnp.cumsum(_to_chunks(log_a, C), axis=-2)
    causal = jnp.tril(jnp.ones((C, C), dtype=bool))

    def step(S, inp):
        qi, ki, vi, bi = inp
        diff = bi[..., :, None, :] - bi[..., None, :, :]
        decay = jnp.exp(jnp.where(causal[..., None], diff, -jnp.inf))
        A = jnp.einsum('bhtd,bhsd,bhtsd->bhts', qi, ki, decay)
        o = jnp.einsum('bhts,bhsv->bhtv', A, vi) + jnp.einsum('bhtd,bhdv->bhtv', qi * jnp.exp(bi), S)
        b_last = bi[..., -1:, :]
        S = S * jnp.exp(b_last)[..., 0, :, None] + jnp.einsum(
            'bhsd,bhsv->bhdv', ki * jnp.exp(b_last - bi), vi)
        return S, o

    S0 = jnp.zeros((B, H, dk, dv), jnp.float32)
    _, o = lax.scan(step, S0, (qc, kc, vc, bc))
    return _from_chunks(o)


def gated_delta_chunked(q, k, v, g, beta):
    B, T, H, dk = q.shape
    dv = v.shape[-1]
    C = GDN_CHUNK
    N = T // C
    qc = _to_chunks(q * dk ** -0.5, C)
    kc = _to_chunks(k, C)
    vc = _to_chunks(v, C)
    gc = jnp.cumsum(g.reshape(B, N, C, H).transpose(1, 0, 3, 2), axis=-1)
    bc = beta.reshape(B, N, C, H).transpose(1, 0, 3, 2)
    incl = jnp.tril(jnp.ones((C, C), dtype=bool))
    strict = jnp.tril(jnp.ones((C, C), dtype=bool), k=-1)
    decay = jnp.exp(jnp.where(incl, gc[..., :, None] - gc[..., None, :], -jnp.inf))
    k_beta = kc * bc[..., None]
    L = jnp.where(strict, jnp.einsum('nbhtd,nbhsd->nbhts', k_beta, kc) * decay, 0.0)
    eye = jnp.eye(C, dtype=L.dtype)
    t_inv = lax.linalg.triangular_solve(L + eye, jnp.broadcast_to(eye, L.shape),
                                        left_side=True, lower=True, unit_diagonal=True)
    u = jnp.einsum('nbhts,nbhsv->nbhtv', t_inv, vc * bc[..., None])
    w = jnp.einsum('nbhts,nbhsd->nbhtd', t_inv, k_beta * jnp.exp(gc)[..., None])
    attn = jnp.where(incl, jnp.einsum('nbhtd,nbhsd->nbhts', qc, kc) * decay, 0.0)

    def step(S, inp):
        qi, ki, ui, wi, gi, ai = inp
        v_new = ui - jnp.einsum('bhtd,bhdv->bhtv', wi, S)
        o = jnp.einsum('bhtd,bhdv->bhtv', qi * jnp.exp(gi)[..., None], S) + jnp.einsum(
            'bhts,bhsv->bhtv', ai, v_new)
        g_last = gi[..., -1:]
        S = S * jnp.exp(g_last)[..., None] + jnp.einsum(
            'bhsd,bhsv->bhdv', ki * jnp.exp(g_last - gi)[..., None], v_new)
        return S, o

    S0 = jnp.zeros((B, H, dk, dv), jnp.float32)
    _, o = lax.scan(step, S0, (qc, kc, u, w, gc, attn))
    return _from_chunks(o)


def sliding_window_attention(q, k, v, sinks):
    B, T, _, hd = q.shape
    W = WINDOW
    nb = T // W
    G = SWA_HEADS // SWA_KV_HEADS
    qb = q.reshape(B, nb, W, SWA_KV_HEADS, G, hd)
    kb = k.reshape(B, nb, W, SWA_KV_HEADS, hd)
    vb = v.reshape(B, nb, W, SWA_KV_HEADS, hd)
    zeros = jnp.zeros_like(kb[:, :1])
    kk = jnp.concatenate([jnp.concatenate([zeros, kb[:, :-1]], axis=1), kb], axis=2)
    vv = jnp.concatenate([jnp.concatenate([zeros, vb[:, :-1]], axis=1), vb], axis=2)
    s = jnp.einsum('bnqhgd,bnkhd->bnhgqk', qb, kk,
                   preferred_element_type=jnp.float32) * (hd ** -0.5)
    q_loc = jnp.arange(W)[:, None] + W
    k_loc = jnp.arange(2 * W)[None, :]
    dist = q_loc - k_loc
    band = (dist >= 0) & (dist < WINDOW)
    has_prev = (jnp.arange(nb)[:, None, None] > 0) | (k_loc >= W)[None]
    valid = band[None] & has_prev
    s = jnp.where(valid[None, :, None, None], s, -jnp.inf)
    sink = sinks.astype(jnp.float32).reshape(SWA_KV_HEADS, G)[None, None, :, :, None, None]
    m = jnp.maximum(jnp.max(s, axis=-1, keepdims=True), sink)
    p = jnp.exp(s - m)
    probs = p / (jnp.sum(p, axis=-1, keepdims=True) + jnp.exp(sink - m))
    o = jnp.einsum('bnhgqk,bnkhd->bnqhgd', probs.astype(v.dtype), vv)
    return o.reshape(B, T, SWA_HEADS * hd)


def causal_depthwise_conv(t, w):
    ch = t.shape[-1]
    return lax.conv_general_dilated(
        t, w.astype(t.dtype)[:, None, :], window_strides=(1,), padding=((CONV_WIDTH - 1, 0),),
        dimension_numbers=('NWC', 'WIO', 'NWC'), feature_group_count=ch)


def hybrid_token_mixer(h, positions, w_in, gla_w_gk, gla_b_gk, gla_norm_gain, swa_sinks,
                       gdn_conv_w, gdn_a_log, gdn_dt_bias, gdn_norm_gain, w_out):
    B, T, _ = h.shape
    f32 = jnp.float32
    proj = h @ w_in
    (aq, ak, av, ag, alr, bq, bk, bv, cq, ck, cv, cz, cb, ca) = _split_columns(proj, IN_SIZES)

    log_a = jax.nn.log_sigmoid((alr @ gla_w_gk + gla_b_gk).astype(f32)) / GLA_NORMALIZER
    o_a = gla_chunked(aq.astype(f32).reshape(B, T, GLA_HEADS, GLA_DK),
                      ak.astype(f32).reshape(B, T, GLA_HEADS, GLA_DK),
                      av.astype(f32).reshape(B, T, GLA_HEADS, GLA_DV),
                      log_a.reshape(B, T, GLA_HEADS, GLA_DK))
    o_a = rms_norm(o_a, gla_norm_gain) * jax.nn.silu(ag.astype(f32).reshape(B, T, GLA_HEADS, GLA_DV))

    qs = partial_rotary(bq.reshape(B, T, SWA_HEADS, SWA_HD), positions)
    ks = partial_rotary(bk.reshape(B, T, SWA_KV_HEADS, SWA_HD), positions)
    o_b = sliding_window_attention(qs, ks, bv.reshape(B, T, SWA_KV_HEADS, SWA_HD), swa_sinks)

    qkv = jax.nn.silu(causal_depthwise_conv(jnp.concatenate([cq, ck, cv], axis=-1), gdn_conv_w))
    cq, ck, cv = _split_columns(qkv.astype(f32), (GDN_HEADS * GDN_DK, GDN_HEADS * GDN_DK, GDN_HEADS * GDN_DV))
    beta = jax.nn.sigmoid(cb.astype(f32))
    g = -jnp.exp(gdn_a_log.astype(f32)) * jax.nn.softplus(ca.astype(f32) + gdn_dt_bias.astype(f32))
    o_c = gated_delta_chunked(l2_normalize(cq.reshape(B, T, GDN_HEADS, GDN_DK)),
                              l2_normalize(ck.reshape(B, T, GDN_HEADS, GDN_DK)),
                              cv.reshape(B, T, GDN_HEADS, GDN_DV), g, beta)
    o_c = rms_norm(o_c, gdn_norm_gain) * jax.nn.silu(cz.astype(f32).reshape(B, T, GDN_HEADS, GDN_DV))

    mix = jnp.concatenate([o_a.reshape(B, T, -1).astype(h.dtype), o_b.astype(h.dtype),
                           o_c.reshape(B, T, -1).astype(h.dtype)], axis=-1)
    return mix @ w_out


def swiglu_ffn(h, w_gate, w_up, w_down):
    return (jax.nn.silu(h @ w_gate) * (h @ w_up)) @ w_down


def setup_inputs(seed: int = 0) -> dict:
    key = jax.random.key(seed)
    ks = jax.random.split(key, 24)
    f32 = jnp.float32
    D = D_MODEL

    def nrm(k, shape, fan_in, scale=1.0):
        return jax.random.normal(k, shape, f32) * (scale * fan_in ** -0.5)

    def gain(k, shape):
        return 1.0 + 0.02 * jax.random.normal(k, shape, f32)

    x = jax.random.normal(ks[0], (BATCH, SEQ, D), f32)
    c = jax.random.normal(ks[1], (BATCH, D), f32)
    positions = (jax.random.randint(ks[2], (BATCH, 1), 0, 1024, dtype=jnp.int32)
                 + jnp.arange(SEQ, dtype=jnp.int32)[None, :])
    w_mod = nrm(ks[3], (DEPTH, D, 6 * D), D, 0.5)
    b_mod = 0.02 * jax.random.normal(ks[4], (DEPTH, 6 * D), f32)
    norm1_gain = gain(ks[5], (DEPTH, D))
    norm2_gain = gain(ks[6], (DEPTH, D))
    w_in = nrm(ks[7], (DEPTH, D, IN_WIDTH), D)
    gla_w_gk = nrm(ks[8], (DEPTH, GLA_RANK, GLA_HEADS * GLA_DK), GLA_RANK)
    gla_b_gk = 0.1 * jax.random.normal(ks[9], (DEPTH, GLA_HEADS * GLA_DK), f32)
    gla_norm_gain = gain(ks[10], (DEPTH, GLA_DV))
    swa_sinks = jax.random.normal(ks[11], (DEPTH, SWA_HEADS), f32)
    gdn_conv_w = nrm(ks[12], (DEPTH, CONV_WIDTH, GDN_CONV_CH), CONV_WIDTH)
    gdn_a_log = jnp.log(jax.random.uniform(ks[13], (DEPTH, GDN_HEADS), f32, 1.0, 16.0))
    dt = jnp.exp(jax.random.uniform(ks[14], (DEPTH, GDN_HEADS), f32, float(np.log(1e-3)), float(np.log(1e-1))))
    gdn_dt_bias = dt + jnp.log(-jnp.expm1(-dt))
    gdn_norm_gain = gain(ks[15], (DEPTH, GDN_DV))
    w_out = nrm(ks[16], (DEPTH, MIX_WIDTH, D), MIX_WIDTH)
    ffn_w_gate = nrm(ks[17], (DEPTH, D, D_FF), D)
    ffn_w_up = nrm(ks[18], (DEPTH, D, D_FF), D)
    ffn_w_down = nrm(ks[19], (DEPTH, D_FF, D), D_FF)
    final_norm_gain = gain(ks[20], (D,))
    return {"x": x, "c": c, "positions": positions, "w_mod": w_mod, "b_mod": b_mod,
            "norm1_gain": norm1_gain, "norm2_gain": norm2_gain, "w_in": w_in,
            "gla_w_gk": gla_w_gk, "gla_b_gk": gla_b_gk, "gla_norm_gain": gla_norm_gain,
            "swa_sinks": swa_sinks, "gdn_conv_w": gdn_conv_w, "gdn_a_log": gdn_a_log,
            "gdn_dt_bias": gdn_dt_bias, "gdn_norm_gain": gdn_norm_gain, "w_out": w_out,
            "ffn_w_gate": ffn_w_gate, "ffn_w_up": ffn_w_up, "ffn_w_down": ffn_w_down,
            "final_norm_gain": final_norm_gain}


def reference(x, c, positions, w_mod, b_mod, norm1_gain, norm2_gain, w_in, gla_w_gk, gla_b_gk,
              gla_norm_gain, swa_sinks, gdn_conv_w, gdn_a_log, gdn_dt_bias, gdn_norm_gain, w_out,
              ffn_w_gate, ffn_w_up, ffn_w_down, final_norm_gain):
    c_act = jax.nn.silu(c)
    for l in range(DEPTH):
        mod = c_act @ w_mod[l] + b_mod[l]
        shift1, scale1, gate1, shift2, scale2, gate2 = jnp.split(mod[:, None, :], 6, axis=-1)
        h = rms_norm(x, norm1_gain[l]) * (1.0 + scale1) + shift1
        x = x + gate1 * hybrid_token_mixer(h, positions, w_in[l], gla_w_gk[l], gla_b_gk[l],
                                           gla_norm_gain[l], swa_sinks[l], gdn_conv_w[l],
                                           gdn_a_log[l], gdn_dt_bias[l], gdn_norm_gain[l], w_out[l])
        h = rms_norm(x, norm2_gain[l]) * (1.0 + scale2) + shift2
        x = x + gate2 * swiglu_ffn(h, ffn_w_gate[l], ffn_w_up[l], ffn_w_down[l])
    return rms_norm(x, final_norm_gain)
```

```python
import functools

import numpy as np
import jax
import jax.numpy as jnp
from jax import lax
from jax.experimental import pallas as pl
from jax.experimental.pallas import tpu as pltpu

f32 = jnp.float32
bf16 = jnp.bfloat16
_HI = lax.Precision.HIGHEST

D_MODEL = 2048
EPS = 1e-6
GLA_HEADS, GLA_DK, GLA_DV, GLA_RANK = 4, 64, 128, 16
GLA_NORMALIZER = 16.0
SWA_HEADS, SWA_KV_HEADS, SWA_HD, WINDOW = 16, 2, 64, 128
ROPE_THETA, ROPE_DIMS = 500000.0, 16
GDN_HEADS, GDN_DK, GDN_DV, CONV_WIDTH = 4, 128, 128, 4
D_FF = 5632
CHUNK = 64
LANES = 128

COL_BQ, COL_AV, COL_AG = 0, 1024, 1536
COL_CQ, COL_CK, COL_CV, COL_CZ = 2048, 2560, 3072, 3584
COL_AQ, COL_AK, COL_BKV, COL_SMALL = 4096, 4352, 4608, 4864
PROJ_WIDTH = 5120
SM_ALR, SM_CB, SM_CA = 0, 16, 20

VMEM_LIMIT = 56 * 1024 * 1024


def _cp(*sem):
    return pltpu.CompilerParams(dimension_semantics=sem, vmem_limit_bytes=VMEM_LIMIT)


def _dot(a, b):
    return jnp.dot(a, b, preferred_element_type=f32)


def _dot_nt(a, b):
    return lax.dot_general(a, b, (((1,), (1,)), ((), ())), preferred_element_type=f32)


def _dot_tn(a, b):
    return lax.dot_general(a, b, (((0,), (0,)), ((), ())), preferred_element_type=f32)


def _dot_hi(a, b):
    return jnp.dot(a, b, preferred_element_type=f32, precision=_HI)


def _silu(x):
    return x * jax.nn.sigmoid(x)


def _norm_mod_rows(x_ref, gain, scale, shift, h_ref, rows):
    n = x_ref.shape[0] // rows

    def body(i, carry):
        sl = pl.ds(pl.multiple_of(i * rows, rows), rows)
        x = x_ref[sl, :]
        ms = jnp.mean(x * x, axis=-1, keepdims=True)
        y = (x * lax.rsqrt(ms + EPS)) * gain
        h_ref[sl, :] = (y * (1.0 + scale) + shift).astype(h_ref.dtype)
        return carry

    lax.fori_loop(0, n, body, 0)


def _mod_kernel(c_ref, w_ref, b_ref, o_ref):
    c = c_ref[...]
    o_ref[0] = _dot(_silu(c).astype(bf16), w_ref[0].astype(bf16)) + b_ref[0]


def _modulation(c, w_mod, b_mod):
    depth, d, n = w_mod.shape
    bsz = c.shape[0]
    rows = 8
    c_pad = jnp.pad(c, ((0, rows - bsz), (0, 0)))
    tn = 1024
    out = pl.pallas_call(
        _mod_kernel,
        grid=(depth, n // tn),
        in_specs=[
            pl.BlockSpec((rows, d), lambda l, j: (0, 0)),
            pl.BlockSpec((1, d, tn), lambda l, j: (l, 0, j)),
            pl.BlockSpec((1, 1, tn), lambda l, j: (l, 0, j)),
        ],
        out_specs=pl.BlockSpec((1, rows, tn), lambda l, j: (l, 0, j)),
        out_shape=jax.ShapeDtypeStruct((depth, rows, n), f32),
        compiler_params=_cp("parallel", "parallel"),
        name="modulation",
    )(c_pad, w_mod, b_mod.reshape(depth, 1, n))
    return out[:, :bsz, :].reshape(depth, bsz, 6, d)


def _in_proj_kernel(x_ref, mod_ref, gain_ref, w_ref, o_ref, h_ref):
    @pl.when(pl.program_id(1) == 0)
    def _():
        _norm_mod_rows(x_ref, gain_ref[...], mod_ref[0, 1:2, :], mod_ref[0, 0:1, :], h_ref, 128)

    o_ref[...] = _dot(h_ref[...], w_ref[...])


def _in_proj(x2, mod_l, gain, w, seq, tm=1024, tn=1024):
    m, d = x2.shape
    n = w.shape[1]
    per_b = seq // tm
    return pl.pallas_call(
        _in_proj_kernel,
        grid=(m // tm, n // tn),
        in_specs=[
            pl.BlockSpec((tm, d), lambda i, j: (i, 0)),
            pl.BlockSpec((1, 6, d), lambda i, j: (i // per_b, 0, 0)),
            pl.BlockSpec((1, d), lambda i, j: (0, 0)),
            pl.BlockSpec((d, tn), lambda i, j: (0, j)),
        ],
        out_specs=pl.BlockSpec((tm, tn), lambda i, j: (i, j)),
        out_shape=jax.ShapeDtypeStruct((m, n), f32),
        scratch_shapes=[pltpu.VMEM((tm, d), bf16)],
        compiler_params=_cp("parallel", "arbitrary"),
        name="in_proj",
    )(x2, mod_l, gain.reshape(1, d), w)


def _out_proj_kernel(x_ref, oa_ref, ob_ref, oc_ref, wa_ref, wb_ref, wc_ref, mod_ref, o_ref):
    acc = _dot(oa_ref[...], wa_ref[...])
    acc = acc + _dot(ob_ref[...], wb_ref[...])
    acc = acc + _dot(oc_ref[...], wc_ref[...])
    o_ref[...] = x_ref[...] + mod_ref[0, 2:3, :] * acc


def _out_proj(x2, oa, ob, oc, w_out, mod_l, seq, tm=512):
    m, d = x2.shape
    per_b = seq // tm
    ka, kb, kc = oa.shape[1], ob.shape[1], oc.shape[1]
    return pl.pallas_call(
        _out_proj_kernel,
        grid=(m // tm,),
        in_specs=[
            pl.BlockSpec((tm, d), lambda i: (i, 0)),
            pl.BlockSpec((tm, ka), lambda i: (i, 0)),
            pl.BlockSpec((tm, kb), lambda i: (i, 0)),
            pl.BlockSpec((tm, kc), lambda i: (i, 0)),
            pl.BlockSpec((ka, d), lambda i: (0, 0)),
            pl.BlockSpec((kb, d), lambda i: (0, 0)),
            pl.BlockSpec((kc, d), lambda i: (0, 0)),
            pl.BlockSpec((1, 6, d), lambda i: (i // per_b, 0, 0)),
        ],
        out_specs=pl.BlockSpec((tm, d), lambda i: (i, 0)),
        out_shape=jax.ShapeDtypeStruct((m, d), f32),
        compiler_params=_cp("parallel"),
        name="out_proj",
    )(x2, oa, ob, oc, w_out[:ka], w_out[ka:ka + kb], w_out[ka + kb:], mod_l)


def _ffn_kernel(x_ref, mod_ref, gain_ref, fgain_ref, wg_ref, wu_ref, wd_ref, o_ref, h_ref, acc_ref, *, final_norm):
    f = pl.program_id(1)

    @pl.when(f == 0)
    def _():
        _norm_mod_rows(x_ref, gain_ref[...], mod_ref[0, 4:5, :], mod_ref[0, 3:4, :], h_ref, 128)
        acc_ref[...] = jnp.zeros_like(acc_ref)

    h = h_ref[...]
    g = _dot(h, wg_ref[...])
    u = _dot(h, wu_ref[...])
    a = (_silu(g) * u).astype(bf16)
    acc_ref[...] += _dot(a, wd_ref[...])

    @pl.when(f == pl.num_programs(1) - 1)
    def _():
        y = x_ref[...] + mod_ref[0, 5:6, :] * acc_ref[...]
        if final_norm:
            ms = jnp.mean(y * y, axis=-1, keepdims=True)
            y = (y * lax.rsqrt(ms + EPS)) * fgain_ref[...]
        o_ref[...] = y


def _ffn(x2, mod_l, gain, fgain, wg, wu, wd, seq, final_norm, tm=512, tf=512):
    m, d = x2.shape
    dff = wg.shape[1]
    per_b = seq // tm
    return pl.pallas_call(
        functools.partial(_ffn_kernel, final_norm=final_norm),
        grid=(m // tm, dff // tf),
        in_specs=[
            pl.BlockSpec((tm, d), lambda i, j: (i, 0)),
            pl.BlockSpec((1, 6, d), lambda i, j: (i // per_b, 0, 0)),
            pl.BlockSpec((1, d), lambda i, j: (0, 0)),
            pl.BlockSpec((1, d), lambda i, j: (0, 0)),
            pl.BlockSpec((d, tf), lambda i, j: (0, j)),
            pl.BlockSpec((d, tf), lambda i, j: (0, j)),
            pl.BlockSpec((tf, d), lambda i, j: (j, 0)),
        ],
        out_specs=pl.BlockSpec((tm, d), lambda i, j: (i, 0)),
        out_shape=jax.ShapeDtypeStruct((m, d), f32),
        scratch_shapes=[pltpu.VMEM((tm, d), bf16), pltpu.VMEM((tm, d), f32)],
        compiler_params=_cp("parallel", "arbitrary"),
        name="ffn",
    )(x2, mod_l, gain.reshape(1, d), fgain.reshape(1, d), wg, wu, wd)


def _rope_kernel(pos_ref, invf_ref, sgn_ref, cos_ref, sin_ref):
    ang = pos_ref[...].astype(f32) * invf_ref[...]
    cos_ref[...] = jnp.cos(ang)
    sin_ref[...] = jnp.sin(ang) * sgn_ref[...]


def _rope_tables(positions):
    m = positions.size
    half = ROPE_DIMS // 2
    inv_freq = ROPE_THETA ** (-jnp.arange(half, dtype=f32) / half)
    dim = np.arange(LANES) % SWA_HD
    invf = jnp.where(jnp.asarray(dim < ROPE_DIMS), inv_freq[jnp.asarray(dim % half)], 0.0).reshape(1, LANES)
    sgn = jnp.asarray(np.where(dim < half, -1.0, 1.0), dtype=f32).reshape(1, LANES)
    pos_b = jnp.broadcast_to(positions.reshape(m, 1), (m, LANES))
    tr = 2048
    return pl.pallas_call(
        _rope_kernel,
        grid=(m // tr,),
        in_specs=[
            pl.BlockSpec((tr, LANES), lambda i: (i, 0)),
            pl.BlockSpec((1, LANES), lambda i: (0, 0)),
            pl.BlockSpec((1, LANES), lambda i: (0, 0)),
        ],
        out_specs=[pl.BlockSpec((tr, LANES), lambda i: (i, 0))] * 2,
        out_shape=[jax.ShapeDtypeStruct((m, LANES), f32)] * 2,
        compiler_params=_cp("parallel"),
        name="rope_tables",
    )(pos_b, invf, sgn)


def _swa_kernel(sink_ref, q_ref, kv_ref, cos_ref, sin_ref, o_ref, kvprev_ref):
    blk = pl.program_id(1)
    w = WINDOW

    @pl.when(blk == 0)
    def _():
        kvprev_ref[...] = jnp.zeros_like(kvprev_ref)

    cos = cos_ref[...]
    sin = sin_ref[...]
    lane = lax.broadcasted_iota(jnp.int32, (1, LANES), 1)
    low_half = (lane % SWA_HD) < (ROPE_DIMS // 2)
    first_head = lane < SWA_HD

    def rot(x):
        swapped = jnp.where(low_half, pltpu.roll(x, LANES - ROPE_DIMS // 2, 1), pltpu.roll(x, ROPE_DIMS // 2, 1))
        return x * cos + swapped * sin

    kv = kv_ref[...]
    kcur = rot(kv[:, :LANES])
    vcur = kv[:, LANES:]
    kk = jnp.concatenate([kvprev_ref[:, :LANES], kcur], axis=0)
    vv = jnp.concatenate([kvprev_ref[:, LANES:], vcur], axis=0)
    kk_sw = pltpu.roll(kk, SWA_HD, 1)
    vv_sw = pltpu.roll(vv, SWA_HD, 1)
    kdup = [jnp.where(first_head, kk, kk_sw).astype(bf16), jnp.where(first_head, kk_sw, kk).astype(bf16)]
    vlo = [jnp.where(first_head, vv, 0.0).astype(bf16), jnp.where(first_head, vv_sw, 0.0).astype(bf16)]
    vhi = [jnp.where(first_head, 0.0, vv_sw).astype(bf16), jnp.where(first_head, 0.0, vv).astype(bf16)]

    qi = lax.broadcasted_iota(jnp.int32, (w, 2 * w), 0)
    kj = lax.broadcasted_iota(jnp.int32, (w, 2 * w), 1)
    dist = qi + w - kj
    valid = (dist >= 0) & (dist < w) & ((kj >= w) | (blk > 0))

    group = SWA_HEADS // SWA_KV_HEADS
    for t in range(SWA_HEADS // 2):
        j = (2 * t) // group
        qt = rot(q_ref[:, t * LANES:(t + 1) * LANES]) * (SWA_HD ** -0.5)
        out = None
        for half in range(2):
            h = 2 * t + half
            qh = jnp.where(first_head if half == 0 else ~first_head, qt, 0.0).astype(bf16)
            s = _dot_nt(qh, kdup[j])
            s = jnp.where(valid, s, -jnp.inf)
            sink = sink_ref[h]
            mx = jnp.maximum(jnp.max(s, axis=-1, keepdims=True), sink)
            p = jnp.exp(s - mx)
            den = jnp.sum(p, axis=-1, keepdims=True) + jnp.exp(sink - mx)
            o = _dot(p.astype(bf16), (vlo if half == 0 else vhi)[j]) / den
            out = o if out is None else out + o
        o_ref[:, t * LANES:(t + 1) * LANES] = out.astype(o_ref.dtype)

    kvprev_ref[:, :LANES] = kcur
    kvprev_ref[:, LANES:] = vcur


def _swa(proj, cos_t, sin_t, sinks, bsz, seq):
    m = proj.shape[0]
    nb = seq // WINDOW
    qw = SWA_HEADS * SWA_HD
    kvw = 2 * SWA_KV_HEADS * SWA_HD
    row = lambda b, i: b * nb + i
    return pl.pallas_call(
        _swa_kernel,
        grid=(bsz, nb),
        in_specs=[
            pl.BlockSpec(memory_space=pltpu.SMEM),
            pl.BlockSpec((WINDOW, qw), lambda b, i: (row(b, i), COL_BQ // qw)),
            pl.BlockSpec((WINDOW, kvw), lambda b, i: (row(b, i), COL_BKV // kvw)),
            pl.BlockSpec((WINDOW, LANES), lambda b, i: (row(b, i), 0)),
            pl.BlockSpec((WINDOW, LANES), lambda b, i: (row(b, i), 0)),
        ],
        out_specs=pl.BlockSpec((WINDOW, qw), lambda b, i: (row(b, i), 0)),
        out_shape=jax.ShapeDtypeStruct((m, qw), bf16),
        scratch_shapes=[pltpu.VMEM((WINDOW, kvw), f32)],
        compiler_params=_cp("parallel", "arbitrary"),
        name="swa",
    )(sinks, proj, proj, cos_t, sin_t)


_GLA_LEVELS = (32, 16, 8, 4, 2, 1)


def _gla_constants():
    c = CHUNK
    t = np.arange(c)
    tri = (t[:, None] >= t[None, :]).astype(np.float32)
    sel = np.zeros((len(_GLA_LEVELS) * c, c), np.float32)
    for l, mhalf in enumerate(_GLA_LEVELS):
        mid = (t // (2 * mhalf)) * 2 * mhalf + mhalf - 1
        sel[l * c + t, mid] = 1.0
    n = GLA_HEADS * c
    idx = np.arange(n)
    head, tt = idx // c, idx % c
    same = head[:, None] == head[None, :]
    level = np.full((n, n), -1, np.int32)
    for l, mhalf in enumerate(_GLA_LEVELS):
        blk = (tt[:, None] // (2 * mhalf)) == (tt[None, :] // (2 * mhalf))
        msk = same & blk & ((tt[:, None] % (2 * mhalf)) >= mhalf) & ((tt[None, :] % (2 * mhalf)) < mhalf)
        level[msk] = l
    level[same & (tt[:, None] == tt[None, :])] = len(_GLA_LEVELS)
    return jnp.asarray(tri), jnp.asarray(sel), jnp.asarray(level)


def _stack_heads(x, width, heads):
    return jnp.concatenate([x[:, h * width:(h + 1) * width] for h in range(heads)], axis=0)


def _gla_kernel(q_ref, k_ref, v_ref, g_ref, sm_ref, wgk_ref, bgk_ref, gain_ref, tri_ref, sel_ref, lv_ref,
                o_ref, st_ref):
    c = CHUNK

    @pl.when(pl.program_id(1) == 0)
    def _():
        st_ref[...] = jnp.zeros_like(st_ref)

    lane = lax.broadcasted_iota(jnp.int32, (1, GLA_HEADS * GLA_DK), 1)
    head_mask = [(lane // GLA_DK) == h for h in range(GLA_HEADS)]

    def per_head_rows(x):
        return jnp.concatenate([jnp.where(head_mask[h], x, 0.0) for h in range(GLA_HEADS)], axis=0).astype(bf16)

    def body(i, carry):
        r = pl.ds(pl.multiple_of(i * c, c), c)
        q = q_ref[r, :] * (GLA_DK ** -0.5)
        k = k_ref[r, :]
        x = _dot_hi(sm_ref[r, :], wgk_ref[...]) + bgk_ref[...]
        log_a = jax.nn.log_sigmoid(x) * (1.0 / GLA_NORMALIZER)
        b = _dot_hi(tri_ref[...], log_a)
        bmid = _dot_hi(sel_ref[...], b)
        lv = lv_ref[...]
        a = jnp.zeros((GLA_HEADS * c, GLA_HEADS * c), f32)
        for l in range(len(_GLA_LEVELS) + 1):
            if l < len(_GLA_LEVELS):
                bm = bmid[l * c:(l + 1) * c, :]
                qe = q * jnp.exp(jnp.minimum(b - bm, 0.0))
                ke = k * jnp.exp(jnp.minimum(bm - b, 0.0))
            else:
                qe, ke = q, k
            ke16 = ke.astype(bf16)
            res = _dot_nt(per_head_rows(qe), jnp.concatenate([ke16] * GLA_HEADS, axis=0))
            a = a + jnp.where(lv == l, res, 0.0)
        vst = _stack_heads(v_ref[r, :], GLA_DV, GLA_HEADS).astype(bf16)
        o = _dot(a.astype(bf16), vst)
        st = st_ref[...]
        o = o + _dot_nt(per_head_rows(q * jnp.exp(b)), st.astype(bf16))
        b_last = b[c - 1:c, :]
        kd = per_head_rows(k * jnp.exp(b_last - b))
        st_ref[...] = st * jnp.exp(b_last) + _dot_tn(vst, kd)
        ms = jnp.mean(o * o, axis=-1, keepdims=True)
        y = (o * lax.rsqrt(ms + EPS)) * gain_ref[...]
        y = y * _silu(_stack_heads(g_ref[r, :], GLA_DV, GLA_HEADS))
        for h in range(GLA_HEADS):
            o_ref[r, h * GLA_DV:(h + 1) * GLA_DV] = y[h * c:(h + 1) * c, :].astype(o_ref.dtype)
        return carry

    lax.fori_loop(0, q_ref.shape[0] // c, body, 0)


def _gla(proj, w_gk, b_gk, gain, bsz, seq, tb=256):
    m = proj.shape[0]
    nb = seq // tb
    qk_w = GLA_HEADS * GLA_DK
    v_w = GLA_HEADS * GLA_DV
    tri, sel, level = _gla_constants()
    wgk_pad = jnp.zeros((LANES, qk_w), f32).at[SM_ALR:SM_ALR + GLA_RANK].set(w_gk)
    row = lambda b, i: b * nb + i
    const = lambda shape: pl.BlockSpec(shape, lambda b, i: (0,) * len(shape))
    return pl.pallas_call(
        _gla_kernel,
        grid=(bsz, nb),
        in_specs=[
            pl.BlockSpec((tb, qk_w), lambda b, i: (row(b, i), COL_AQ // qk_w)),
            pl.BlockSpec((tb, qk_w), lambda b, i: (row(b, i), COL_AK // qk_w)),
            pl.BlockSpec((tb, v_w), lambda b, i: (row(b, i), COL_AV // v_w)),
            pl.BlockSpec((tb, v_w), lambda b, i: (row(b, i), COL_AG // v_w)),
            pl.BlockSpec((tb, LANES), lambda b, i: (row(b, i), COL_SMALL // LANES)),
            const((LANES, qk_w)), const((1, qk_w)), const((1, GLA_DV)),
            const(tri.shape), const(sel.shape), const(level.shape),
        ],
        out_specs=pl.BlockSpec((tb, v_w), lambda b, i: (row(b, i), 0)),
        out_shape=jax.ShapeDtypeStruct((m, v_w), bf16),
        scratch_shapes=[pltpu.VMEM((GLA_DV, qk_w), f32)],
        compiler_params=_cp("parallel", "arbitrary"),
        name="gla",
    )(proj, proj, proj, proj, proj, wgk_pad, b_gk.reshape(1, qk_w), gain.reshape(1, GLA_DV), tri, sel, level)


def _gdn_constants():
    c = CHUNK
    n = GDN_HEADS * c
    idx = np.arange(n)
    head, tt = idx // c, idx % c
    same = head[:, None] == head[None, :]
    tri = (same & (tt[:, None] >= tt[None, :])).astype(np.float32)
    kind = np.zeros((n, n), np.int32)
    kind[same & (tt[:, None] == tt[None, :])] = 1
    kind[same & (tt[:, None] > tt[None, :])] = 2
    return jnp.asarray(tri), jnp.asarray(kind)


def _gdn_kernel(alog_ref, dtb_ref, cq_ref, ck_ref, cv_ref, cz_ref, sm_ref, cw_ref, gain_ref, tri_ref, kind_ref,
                o_ref, carry_ref, qkv_ref, s_ref):
    c = CHUNK
    tb = cq_ref.shape[0]
    hw = GDN_HEADS * GDN_DK

    @pl.when(pl.program_id(1) == 0)
    def _():
        carry_ref[...] = jnp.zeros_like(carry_ref)
        s_ref[...] = jnp.zeros_like(s_ref)

    row8 = lax.broadcasted_iota(jnp.int32, (8, LANES), 0)
    for part, src in enumerate((cq_ref, ck_ref, cv_ref)):
        for h in range(GDN_HEADS):
            col = part * hw + h * LANES
            x = src[:, h * LANES:(h + 1) * LANES]
            prev = carry_ref[:, col:col + LANES]
            acc = x * cw_ref[CONV_WIDTH - 1:CONV_WIDTH, col:col + LANES]
            for shift in range(1, CONV_WIDTH):
                xs = pltpu.roll(x, shift, 0)
                top = jnp.where(row8 < shift, pltpu.roll(prev, shift, 0), xs[0:8, :])
                xs = jnp.concatenate([top, xs[8:, :]], axis=0)
                acc = acc + xs * cw_ref[CONV_WIDTH - 1 - shift:CONV_WIDTH - shift, col:col + LANES]
            carry_ref[:, col:col + LANES] = x[tb - 8:tb, :]
            y = _silu(acc)
            if part < 2:
                y = y * lax.rsqrt(jnp.sum(y * y, axis=-1, keepdims=True) + EPS)
            if part == 0:
                y = y * (GDN_DK ** -0.5)
            qkv_ref[:, col:col + LANES] = y

    def body(i, carry):
        r = pl.ds(pl.multiple_of(i * c, c), c)
        sm = sm_ref[r, :]
        g_parts, b_parts = [], []
        for h in range(GDN_HEADS):
            ca = jnp.broadcast_to(sm[:, SM_CA + h:SM_CA + h + 1], (c, LANES))
            cb = jnp.broadcast_to(sm[:, SM_CB + h:SM_CB + h + 1], (c, LANES))
            g_parts.append(-jnp.exp(alog_ref[h]) * jax.nn.softplus(ca + dtb_ref[h]))
            b_parts.append(jax.nn.sigmoid(cb))
        g = jnp.concatenate(g_parts, axis=0)
        beta = jnp.concatenate(b_parts, axis=0)
        gc = _dot_hi(tri_ref[...], g)
        gc_row = gc.T[0:1, :]
        diff = jnp.concatenate([gc, gc], axis=1) - gc_row
        dec = jnp.exp(jnp.minimum(diff, 0.0))
        kind = kind_ref[...]
        qst = _stack_heads(qkv_ref[r, 0:hw], GDN_DK, GDN_HEADS)
        kst = _stack_heads(qkv_ref[r, hw:2 * hw], GDN_DK, GDN_HEADS)
        vst = _stack_heads(qkv_ref[r, 2 * hw:3 * hw], GDN_DV, GDN_HEADS)
        egc = jnp.exp(gc)
        kb = kst * beta
        k16 = kst.astype(bf16)
        m_neg = jnp.where(kind == 2, -(_dot_nt(kb.astype(bf16), k16) * dec), 0.0)
        rhs = jnp.concatenate([vst * beta, kb * egc], axis=1)
        m_hi = m_neg.astype(bf16)
        m_lo = (m_neg - m_hi.astype(f32)).astype(bf16)
        p = m_hi
        tinv = m_neg
        for _ in range(5):
            p = _dot(p, p).astype(bf16)
            tinv = tinv + _dot(tinv.astype(bf16), p) + p.astype(f32)
        t16 = tinv.astype(bf16)
        x0 = (rhs + _dot(t16, rhs.astype(bf16))).astype(bf16)
        x0f = x0.astype(f32)
        resid = (rhs - x0f) + (_dot(m_hi, x0) + _dot(m_lo, x0))
        xs = x0f + (resid + _dot(t16, resid.astype(bf16)))
        q16 = qst.astype(bf16)
        attn = jnp.where(kind >= 1, _dot_nt(q16, k16) * dec, 0.0).astype(bf16)
        qg = qst * egc
        vnew_parts, oq_parts = [], []
        for h in range(GDN_HEADS):
            hs = slice(h * c, (h + 1) * c)
            s16 = s_ref[h].astype(bf16)
            t = _dot(jnp.concatenate([xs[hs, GDN_DV:], qg[hs, :]], axis=0).astype(bf16), s16)
            vnew_parts.append(xs[hs, :GDN_DV] - t[:c])
            oq_parts.append(t[c:])
        vnew = jnp.concatenate(vnew_parts, axis=0)
        v16 = vnew.astype(bf16)
        o = jnp.concatenate(oq_parts, axis=0) + _dot(attn, v16)
        for h in range(GDN_HEADS):
            hs = slice(h * c, (h + 1) * c)
            g_last = gc[(h + 1) * c - 1:(h + 1) * c, :]
            kd = (kst[hs, :] * jnp.exp(g_last - gc[hs, :])).astype(bf16)
            s_ref[h] = s_ref[h] * jnp.exp(g_last) + _dot_tn(kd, v16[hs, :])
        ms = jnp.mean(o * o, axis=-1, keepdims=True)
        y = (o * lax.rsqrt(ms + EPS)) * gain_ref[...]
        y = y * _silu(_stack_heads(cz_ref[r, :], GDN_DV, GDN_HEADS))
        for h in range(GDN_HEADS):
            o_ref[r, h * GDN_DV:(h + 1) * GDN_DV] = y[h * c:(h + 1) * c, :].astype(o_ref.dtype)
        return carry

    lax.fori_loop(0, tb // c, body, 0)


def _gdn(proj, conv_w, a_log, dt_bias, gain, bsz, seq, tb=256):
    m = proj.shape[0]
    nb = seq // tb
    hw = GDN_HEADS * GDN_DK
    tri, kind = _gdn_constants()
    row = lambda b, i: b * nb + i
    const = lambda shape: pl.BlockSpec(shape, lambda b, i: (0,) * len(shape))
    smem = pl.BlockSpec(memory_space=pltpu.SMEM)
    return pl.pallas_call(
        _gdn_kernel,
        grid=(bsz, nb),
        in_specs=[
            smem, smem,
            pl.BlockSpec((tb, hw), lambda b, i: (row(b, i), COL_CQ // hw)),
            pl.BlockSpec((tb, hw), lambda b, i: (row(b, i), COL_CK // hw)),
            pl.BlockSpec((tb, hw), lambda b, i: (row(b, i), COL_CV // hw)),
            pl.BlockSpec((tb, hw), lambda b, i: (row(b, i), COL_CZ // hw)),
            pl.BlockSpec((tb, LANES), lambda b, i: (row(b, i), COL_SMALL // LANES)),
            const((CONV_WIDTH, 3 * hw)), const((1, GDN_DV)), const(tri.shape), const(kind.shape),
        ],
        out_specs=pl.BlockSpec((tb, hw), lambda b, i: (row(b, i), 0)),
        out_shape=jax.ShapeDtypeStruct((m, hw), bf16),
        scratch_shapes=[
            pltpu.VMEM((8, 3 * hw), f32),
            pltpu.VMEM((tb, 3 * hw), f32),
            pltpu.VMEM((GDN_HEADS, GDN_DK, GDN_DV), f32),
        ],
        compiler_params=_cp("parallel", "arbitrary"),
        name="gdn",
    )(a_log, dt_bias, proj, proj, proj, proj, proj, conv_w, gain.reshape(1, GDN_DV), tri, kind)


def _permute_w_in(w_in):
    depth, d, _ = w_in.shape
    pieces = [
        w_in[:, :, 1552:2576],
        w_in[:, :, 512:1536],
        w_in[:, :, 2832:4880],
        w_in[:, :, 0:512],
        w_in[:, :, 2576:2832],
        w_in[:, :, 1536:1552],
        w_in[:, :, 4880:4888],
    ]
    used = sum(p.shape[-1] for p in pieces)
    pieces.append(jnp.zeros((depth, d, PROJ_WIDTH - used), w_in.dtype))
    return jnp.concatenate(pieces, axis=-1).astype(bf16)


def kernel(x, c, positions, w_mod, b_mod, norm1_gain, norm2_gain, w_in, gla_w_gk, gla_b_gk, gla_norm_gain,
           swa_sinks, gdn_conv_w, gdn_a_log, gdn_dt_bias, gdn_norm_gain, w_out, ffn_w_gate, ffn_w_up, ffn_w_down,
           final_norm_gain):
    bsz, seq, d = x.shape
    depth = w_in.shape[0]
    mods = _modulation(c, w_mod, b_mod)
    cos_t, sin_t = _rope_tables(positions)
    w_in_p = _permute_w_in(w_in)
    w_out16 = w_out.astype(bf16)
    wg16, wu16, wd16 = ffn_w_gate.astype(bf16), ffn_w_up.astype(bf16), ffn_w_down.astype(bf16)
    x2 = x.reshape(bsz * seq, d)
    for l in range(depth):
        proj = _in_proj(x2, mods[l], norm1_gain[l], w_in_p[l], seq)
        o_a = _gla(proj, gla_w_gk[l], gla_b_gk[l], gla_norm_gain[l], bsz, seq)
        o_b = _swa(proj, cos_t, sin_t, swa_sinks[l], bsz, seq)
        o_c = _gdn(proj, gdn_conv_w[l], gdn_a_log[l], gdn_dt_bias[l], gdn_norm_gain[l], bsz, seq)
        x2 = _out_proj(x2, o_a, o_b, o_c, w_out16[l], mods[l], seq)
        x2 = _ffn(x2, mods[l], norm2_gain[l], final_norm_gain, wg16[l], wu16[l], wd16[l], seq,
                  final_norm=(l == depth - 1))
    return x2.reshape(bsz, seq, d)
```

```python
import functools

import numpy as np
import jax
import jax.numpy as jnp
from jax import lax
from jax.experimental import pallas as pl
from jax.experimental.pallas import tpu as pltpu

f32 = jnp.float32
bf16 = jnp.bfloat16
_HI = lax.Precision.HIGHEST

D_MODEL = 2048
EPS = 1e-6
GLA_HEADS, GLA_DK, GLA_DV, GLA_RANK = 4, 64, 128, 16
GLA_NORMALIZER = 16.0
SWA_HEADS, SWA_KV_HEADS, SWA_HD, WINDOW = 16, 2, 64, 128
ROPE_THETA, ROPE_DIMS = 500000.0, 16
GDN_HEADS, GDN_DK, GDN_DV, CONV_WIDTH = 4, 128, 128, 4
D_FF = 5632
CHUNK = 64
LANES = 128

COL_BQ, COL_AV, COL_AG = 0, 1024, 1536
COL_CQ, COL_CK, COL_CV, COL_CZ = 2048, 2560, 3072, 3584
COL_AQ, COL_AK, COL_BKV, COL_SMALL = 4096, 4352, 4608, 4864
PROJ_WIDTH = 5120
SM_ALR, SM_CB, SM_CA = 0, 16, 20

VMEM_LIMIT = 56 * 1024 * 1024


def _cp(*sem):
    return pltpu.CompilerParams(dimension_semantics=sem, vmem_limit_bytes=VMEM_LIMIT)


def _dot(a, b):
    return jnp.dot(a, b, preferred_element_type=f32)


def _dot_nt(a, b):
    return lax.dot_general(a, b, (((1,), (1,)), ((), ())), preferred_element_type=f32)


def _dot_tn(a, b):
    return lax.dot_general(a, b, (((0,), (0,)), ((), ())), preferred_element_type=f32)


def _dot_hi(a, b):
    return jnp.dot(a, b, preferred_element_type=f32, precision=_HI)


def _silu(x):
    return x * jax.nn.sigmoid(x)


def _norm_mod_rows(x_ref, gain, scale, shift, h_ref, rows):
    n = x_ref.shape[0] // rows
    gs = gain * (1.0 + scale)

    def body(i, carry):
        sl = pl.ds(pl.multiple_of(i * rows, rows), rows)
        x = x_ref[sl, :]
        ms = jnp.mean(x * x, axis=-1, keepdims=True)
        h_ref[sl, :] = ((x * lax.rsqrt(ms + EPS)) * gs + shift).astype(h_ref.dtype)
        return carry

    lax.fori_loop(0, n, body, 0)


def _mod_kernel(c_ref, w_ref, b_ref, o_ref):
    c = c_ref[...]
    o_ref[0] = _dot(_silu(c).astype(bf16), w_ref[0].astype(bf16)) + b_ref[0]


def _modulation(c, w_mod, b_mod):
    depth, d, n = w_mod.shape
    bsz = c.shape[0]
    rows = 8
    c_pad = jnp.pad(c, ((0, rows - bsz), (0, 0)))
    tn = 1024
    out = pl.pallas_call(
        _mod_kernel,
        grid=(depth, n // tn),
        in_specs=[
            pl.BlockSpec((rows, d), lambda l, j: (0, 0)),
            pl.BlockSpec((1, d, tn), lambda l, j: (l, 0, j)),
            pl.BlockSpec((1, 1, tn), lambda l, j: (l, 0, j)),
        ],
        out_specs=pl.BlockSpec((1, rows, tn), lambda l, j: (l, 0, j)),
        out_shape=jax.ShapeDtypeStruct((depth, rows, n), f32),
        compiler_params=_cp("parallel", "parallel"),
        name="modulation",
    )(c_pad, w_mod, b_mod.reshape(depth, 1, n))
    return out[:, :bsz, :].reshape(depth, bsz, 6, d)


def _in_proj_kernel(x_ref, mod_ref, gain_ref, w_ref, o_ref, h_ref):
    @pl.when(pl.program_id(1) == 0)
    def _():
        _norm_mod_rows(x_ref, gain_ref[...], mod_ref[0, 1:2, :], mod_ref[0, 0:1, :], h_ref, 128)

    o_ref[...] = _dot(h_ref[...], w_ref[...])


def _in_proj(x2, mods, gains, w, l, seq, tm=1024, tn=1024):
    m, d = x2.shape
    n = w.shape[2]
    per_b = seq // tm
    return pl.pallas_call(
        _in_proj_kernel,
        grid=(m // tm, n // tn),
        in_specs=[
            pl.BlockSpec((tm, d), lambda i, j: (i, 0)),
            pl.BlockSpec((None, 1, 6, d), lambda i, j: (l, i // per_b, 0, 0)),
            pl.BlockSpec((None, 1, d), lambda i, j: (l, 0, 0)),
            pl.BlockSpec((None, d, tn), lambda i, j: (l, 0, j)),
        ],
        out_specs=pl.BlockSpec((tm, tn), lambda i, j: (i, j)),
        out_shape=jax.ShapeDtypeStruct((m, n), f32),
        scratch_shapes=[pltpu.VMEM((tm, d), bf16)],
        compiler_params=_cp("parallel", "arbitrary"),
        name="in_proj",
    )(x2, mods, gains, w)


def _out_proj_kernel(x_ref, oa_ref, ob_ref, oc_ref, w_ref, mod_ref, o_ref):
    ka, kb = oa_ref.shape[1], ob_ref.shape[1]
    acc = _dot(oa_ref[...], w_ref[0:ka, :])
    acc = acc + _dot(ob_ref[...], w_ref[ka:ka + kb, :])
    acc = acc + _dot(oc_ref[...], w_ref[ka + kb:, :])
    o_ref[...] = x_ref[...] + mod_ref[0, 2:3, :] * acc


def _out_proj(x2, oa, ob, oc, w_out, mods, l, seq, tm=512):
    m, d = x2.shape
    per_b = seq // tm
    ka, kb, kc = oa.shape[1], ob.shape[1], oc.shape[1]
    return pl.pallas_call(
        _out_proj_kernel,
        grid=(m // tm,),
        in_specs=[
            pl.BlockSpec((tm, d), lambda i: (i, 0)),
            pl.BlockSpec((tm, ka), lambda i: (i, 0)),
            pl.BlockSpec((tm, kb), lambda i: (i, 0)),
            pl.BlockSpec((tm, kc), lambda i: (i, 0)),
            pl.BlockSpec((None, ka + kb + kc, d), lambda i: (l, 0, 0)),
            pl.BlockSpec((None, 1, 6, d), lambda i: (l, i // per_b, 0, 0)),
        ],
        out_specs=pl.BlockSpec((tm, d), lambda i: (i, 0)),
        out_shape=jax.ShapeDtypeStruct((m, d), f32),
        compiler_params=_cp("parallel"),
        name="out_proj",
    )(x2, oa, ob, oc, w_out, mods)


def _ffn_kernel(x_ref, mod_ref, gain_ref, fgain_ref, wg_ref, wu_ref, wd_ref, o_ref, h_ref, acc_ref, *, final_norm):
    f = pl.program_id(1)

    @pl.when(f == 0)
    def _():
        _norm_mod_rows(x_ref, gain_ref[...], mod_ref[0, 4:5, :], mod_ref[0, 3:4, :], h_ref, 128)
        acc_ref[...] = jnp.zeros_like(acc_ref)

    h = h_ref[...]
    g = _dot(h, wg_ref[...])
    u = _dot(h, wu_ref[...])
    a = (_silu(g) * u).astype(bf16)
    acc_ref[...] += _dot(a, wd_ref[...])

    @pl.when(f == pl.num_programs(1) - 1)
    def _():
        y = x_ref[...] + mod_ref[0, 5:6, :] * acc_ref[...]
        if final_norm:
            ms = jnp.mean(y * y, axis=-1, keepdims=True)
            y = (y * lax.rsqrt(ms + EPS)) * fgain_ref[...]
        o_ref[...] = y


def _ffn(x2, mods, gains, fgain, wg, wu, wd, l, seq, final_norm, tm=512, tf=512):
    m, d = x2.shape
    dff = wg.shape[2]
    per_b = seq // tm
    return pl.pallas_call(
        functools.partial(_ffn_kernel, final_norm=final_norm),
        grid=(m // tm, dff // tf),
        in_specs=[
            pl.BlockSpec((tm, d), lambda i, j: (i, 0)),
            pl.BlockSpec((None, 1, 6, d), lambda i, j: (l, i // per_b, 0, 0)),
            pl.BlockSpec((None, 1, d), lambda i, j: (l, 0, 0)),
            pl.BlockSpec((1, d), lambda i, j: (0, 0)),
            pl.BlockSpec((None, d, tf), lambda i, j: (l, 0, j)),
            pl.BlockSpec((None, d, tf), lambda i, j: (l, 0, j)),
            pl.BlockSpec((None, tf, d), lambda i, j: (l, j, 0)),
        ],
        out_specs=pl.BlockSpec((tm, d), lambda i, j: (i, 0)),
        out_shape=jax.ShapeDtypeStruct((m, d), f32),
        scratch_shapes=[pltpu.VMEM((tm, d), bf16), pltpu.VMEM((tm, d), f32)],
        compiler_params=_cp("parallel", "arbitrary"),
        name="ffn",
    )(x2, mods, gains, fgain.reshape(1, d), wg, wu, wd)


def _rope_kernel(pos_ref, invf_ref, sgn_ref, cos_ref, sin_ref):
    ang = pos_ref[...].astype(f32) * invf_ref[...]
    cos_ref[...] = jnp.cos(ang)
    sin_ref[...] = jnp.sin(ang) * sgn_ref[...]


def _rope_tables(positions):
    m = positions.size
    half = ROPE_DIMS // 2
    inv_freq = ROPE_THETA ** (-jnp.arange(half, dtype=f32) / half)
    dim = np.arange(LANES) % SWA_HD
    invf = jnp.where(jnp.asarray(dim < ROPE_DIMS), inv_freq[jnp.asarray(dim % half)], 0.0).reshape(1, LANES)
    sgn = jnp.asarray(np.where(dim < half, -1.0, 1.0), dtype=f32).reshape(1, LANES)
    pos_b = jnp.broadcast_to(positions.reshape(m, 1), (m, LANES))
    tr = 2048
    return pl.pallas_call(
        _rope_kernel,
        grid=(m // tr,),
        in_specs=[
            pl.BlockSpec((tr, LANES), lambda i: (i, 0)),
            pl.BlockSpec((1, LANES), lambda i: (0, 0)),
            pl.BlockSpec((1, LANES), lambda i: (0, 0)),
        ],
        out_specs=[pl.BlockSpec((tr, LANES), lambda i: (i, 0))] * 2,
        out_shape=[jax.ShapeDtypeStruct((m, LANES), f32)] * 2,
        compiler_params=_cp("parallel"),
        name="rope_tables",
    )(pos_b, invf, sgn)


def _swa_kernel(sink_ref, q_ref, kv_ref, cos_ref, sin_ref, o_ref, kvprev_ref):
    blk = pl.program_id(1)
    w = WINDOW

    @pl.when(blk == 0)
    def _():
        kvprev_ref[...] = jnp.zeros_like(kvprev_ref)

    cos = cos_ref[...]
    sin = sin_ref[...]
    lane = lax.broadcasted_iota(jnp.int32, (1, LANES), 1)
    low_half = (lane % SWA_HD) < (ROPE_DIMS // 2)
    first_head = lane < SWA_HD

    def rot(x):
        swapped = jnp.where(low_half, pltpu.roll(x, LANES - ROPE_DIMS // 2, 1), pltpu.roll(x, ROPE_DIMS // 2, 1))
        return x * cos + swapped * sin

    kv = kv_ref[...]
    kcur = rot(kv[:, :LANES])
    vcur = kv[:, LANES:]
    kk = jnp.concatenate([kvprev_ref[:, :LANES], kcur], axis=0)
    vv = jnp.concatenate([kvprev_ref[:, LANES:], vcur], axis=0)
    kk_sw = pltpu.roll(kk, SWA_HD, 1)
    vv_sw = pltpu.roll(vv, SWA_HD, 1)
    kdup = [jnp.where(first_head, kk, kk_sw).astype(bf16), jnp.where(first_head, kk_sw, kk).astype(bf16)]
    vlo = [jnp.where(first_head, vv, 0.0).astype(bf16), jnp.where(first_head, vv_sw, 0.0).astype(bf16)]
    vhi = [jnp.where(first_head, 0.0, vv_sw).astype(bf16), jnp.where(first_head, 0.0, vv).astype(bf16)]

    qi = lax.broadcasted_iota(jnp.int32, (w, 2 * w), 0)
    kj = lax.broadcasted_iota(jnp.int32, (w, 2 * w), 1)
    dist = qi + w - kj
    valid = (dist >= 0) & (dist < w) & ((kj >= w) | (blk > 0))

    group = SWA_HEADS // SWA_KV_HEADS
    for t in range(SWA_HEADS // 2):
        j = (2 * t) // group
        qt = rot(q_ref[:, t * LANES:(t + 1) * LANES]) * (SWA_HD ** -0.5)
        out = None
        for half in range(2):
            h = 2 * t + half
            qh = jnp.where(first_head if half == 0 else ~first_head, qt, 0.0).astype(bf16)
            s = _dot_nt(qh, kdup[j])
            s = jnp.where(valid, s, -jnp.inf)
            sink = sink_ref[h]
            mx = jnp.maximum(jnp.max(s, axis=-1, keepdims=True), sink)
            p = jnp.exp(s - mx)
            den = jnp.sum(p, axis=-1, keepdims=True) + jnp.exp(sink - mx)
            o = _dot(p.astype(bf16), (vlo if half == 0 else vhi)[j]) / den
            out = o if out is None else out + o
        o_ref[:, t * LANES:(t + 1) * LANES] = out.astype(o_ref.dtype)

    kvprev_ref[:, :LANES] = kcur
    kvprev_ref[:, LANES:] = vcur


def _swa(proj, cos_t, sin_t, sinks, bsz, seq):
    m = proj.shape[0]
    nb = seq // WINDOW
    qw = SWA_HEADS * SWA_HD
    kvw = 2 * SWA_KV_HEADS * SWA_HD
    row = lambda b, i: b * nb + i
    return pl.pallas_call(
        _swa_kernel,
        grid=(bsz, nb),
        in_specs=[
            pl.BlockSpec(memory_space=pltpu.SMEM),
            pl.BlockSpec((WINDOW, qw), lambda b, i: (row(b, i), COL_BQ // qw)),
            pl.BlockSpec((WINDOW, kvw), lambda b, i: (row(b, i), COL_BKV // kvw)),
            pl.BlockSpec((WINDOW, LANES), lambda b, i: (row(b, i), 0)),
            pl.BlockSpec((WINDOW, LANES), lambda b, i: (row(b, i), 0)),
        ],
        out_specs=pl.BlockSpec((WINDOW, qw), lambda b, i: (row(b, i), 0)),
        out_shape=jax.ShapeDtypeStruct((m, qw), bf16),
        scratch_shapes=[pltpu.VMEM((WINDOW, kvw), f32)],
        compiler_params=_cp("parallel", "arbitrary"),
        name="swa",
    )(sinks, proj, proj, cos_t, sin_t)


_GLA_LEVELS = (32, 16, 8, 4, 2, 1)


def _gla_constants():
    c = CHUNK
    t = np.arange(c)
    tri = (t[:, None] >= t[None, :]).astype(np.float32)
    sel = np.zeros((len(_GLA_LEVELS) * c, c), np.float32)
    for l, mhalf in enumerate(_GLA_LEVELS):
        mid = (t // (2 * mhalf)) * 2 * mhalf + mhalf - 1
        sel[l * c + t, mid] = 1.0
    cum_sel = np.concatenate([tri, sel @ tri], axis=0)
    n = GLA_HEADS * c
    idx = np.arange(n)
    head, tt = idx // c, idx % c
    same = head[:, None] == head[None, :]
    level = np.full((n, n), -1, np.int32)
    for l, mhalf in enumerate(_GLA_LEVELS):
        blk = (tt[:, None] // (2 * mhalf)) == (tt[None, :] // (2 * mhalf))
        msk = same & blk & ((tt[:, None] % (2 * mhalf)) >= mhalf) & ((tt[None, :] % (2 * mhalf)) < mhalf)
        level[msk] = l
    level[same & (tt[:, None] == tt[None, :])] = len(_GLA_LEVELS)
    return jnp.asarray(cum_sel, dtype=bf16), jnp.asarray(level)


def _stack_heads(x, width, heads):
    return jnp.concatenate([x[:, h * width:(h + 1) * width] for h in range(heads)], axis=0)


def _split_hi_lo(x):
    hi = x.astype(bf16)
    return hi, (x - hi.astype(f32)).astype(bf16)


def _gla_kernel(q_ref, k_ref, v_ref, g_ref, sm_ref, wgk_ref, bgk_ref, gain_ref, cumsel_ref, lv_ref, o_ref, st_ref):
    c = CHUNK

    @pl.when(pl.program_id(1) == 0)
    def _():
        st_ref[...] = jnp.zeros_like(st_ref)

    lane = lax.broadcasted_iota(jnp.int32, (1, GLA_HEADS * GLA_DK), 1)
    head_mask = [(lane // GLA_DK) == h for h in range(GLA_HEADS)]

    def per_head_rows(x):
        x16 = x.astype(bf16)
        return jnp.concatenate([jnp.where(head_mask[h], x16, jnp.zeros_like(x16)) for h in range(GLA_HEADS)], axis=0)

    log_a_all = jax.nn.log_sigmoid(_dot(sm_ref[...].astype(bf16), wgk_ref[...]) + bgk_ref[...]) * (1.0 / GLA_NORMALIZER)
    cum_sel = cumsel_ref[...]
    lv = lv_ref[...]
    subs = range(q_ref.shape[0] // c)
    rows = [slice(i * c, (i + 1) * c) for i in subs]
    pair = 2 * c
    qs = [q_ref[r, :] * (GLA_DK ** -0.5) for r in rows]
    ks = [k_ref[r, :] for r in rows]
    bbs = []
    for r in rows:
        la_hi, la_lo = _split_hi_lo(log_a_all[r, :])
        bbs.append(_dot(cum_sel, la_hi) + _dot(cum_sel, la_lo))
    bs = [bb[:c, :] for bb in bbs]
    accs = [[jnp.zeros((c, pair), f32) for _ in range(GLA_HEADS)] for _ in subs]
    for l in range(len(_GLA_LEVELS) + 1):
        for i in subs:
            q, k, b = qs[i], ks[i], bs[i]
            if l < len(_GLA_LEVELS):
                bm = bbs[i][(l + 1) * c:(l + 2) * c, :]
                qe = q * jnp.exp(jnp.minimum(b - bm, 0.0))
                ke = k * jnp.exp(jnp.minimum(bm - b, 0.0))
            else:
                qe, ke = q, k
            ke16 = ke.astype(bf16)
            res = _dot_nt(per_head_rows(qe), jnp.concatenate([ke16] * GLA_HEADS, axis=0))
            for h in range(GLA_HEADS):
                hr = slice(h * c, (h + 1) * c)
                hc = slice((h // 2) * pair, (h // 2 + 1) * pair)
                accs[i][h] = jnp.where(lv[hr, hc] == l, res[hr, hc], accs[i][h])
    vsts = [_stack_heads(v_ref[r, :], GLA_DV, GLA_HEADS).astype(bf16) for r in rows]
    o_intra = []
    for i in subs:
        parts = []
        for hp in range(GLA_HEADS // 2):
            a_pair = jnp.concatenate([accs[i][2 * hp], accs[i][2 * hp + 1]], axis=0).astype(bf16)
            parts.append(_dot(a_pair, vsts[i][hp * pair:(hp + 1) * pair, :]))
        o_intra.append(jnp.concatenate(parts, axis=0))
    qgs = [per_head_rows(q * jnp.exp(b)) for q, b in zip(qs, bs)]
    kds = [per_head_rows(k * jnp.exp(b[c - 1:c, :] - b)) for k, b in zip(ks, bs)]

    for i in subs:
        r, vst = rows[i], vsts[i]
        st = st_ref[...]
        o = o_intra[i] + _dot_nt(qgs[i], st.astype(bf16))
        st_ref[...] = st * jnp.exp(bs[i][c - 1:c, :]) + _dot_tn(vst, kds[i])
        ms = jnp.mean(o * o, axis=-1, keepdims=True)
        y = (o * lax.rsqrt(ms + EPS)) * gain_ref[...]
        y = y * _silu(_stack_heads(g_ref[r, :], GLA_DV, GLA_HEADS))
        for h in range(GLA_HEADS):
            o_ref[r, h * GLA_DV:(h + 1) * GLA_DV] = y[h * c:(h + 1) * c, :].astype(o_ref.dtype)


def _gla(proj, w_gk, b_gk, gain, bsz, seq, tb=256):
    m = proj.shape[0]
    nb = seq // tb
    qk_w = GLA_HEADS * GLA_DK
    v_w = GLA_HEADS * GLA_DV
    cum_sel, level = _gla_constants()
    wgk_pad = jnp.zeros((LANES, qk_w), f32).at[SM_ALR:SM_ALR + GLA_RANK].set(w_gk).astype(bf16)
    row = lambda b, i: b * nb + i
    const = lambda shape: pl.BlockSpec(shape, lambda b, i: (0,) * len(shape))
    return pl.pallas_call(
        _gla_kernel,
        grid=(bsz, nb),
        in_specs=[
            pl.BlockSpec((tb, qk_w), lambda b, i: (row(b, i), COL_AQ // qk_w)),
            pl.BlockSpec((tb, qk_w), lambda b, i: (row(b, i), COL_AK // qk_w)),
            pl.BlockSpec((tb, v_w), lambda b, i: (row(b, i), COL_AV // v_w)),
            pl.BlockSpec((tb, v_w), lambda b, i: (row(b, i), COL_AG // v_w)),
            pl.BlockSpec((tb, LANES), lambda b, i: (row(b, i), COL_SMALL // LANES)),
            const((LANES, qk_w)), const((1, qk_w)), const((1, GLA_DV)),
            const(cum_sel.shape), const(level.shape),
        ],
        out_specs=pl.BlockSpec((tb, v_w), lambda b, i: (row(b, i), 0)),
        out_shape=jax.ShapeDtypeStruct((m, v_w), bf16),
        scratch_shapes=[pltpu.VMEM((GLA_DV, qk_w), f32)],
        compiler_params=_cp("parallel", "arbitrary"),
        name="gla",
    )(proj, proj, proj, proj, proj, wgk_pad, b_gk.reshape(1, qk_w), gain.reshape(1, GLA_DV), cum_sel, level)


def _gdn_constants():
    c = CHUNK
    n = GDN_HEADS * c
    idx = np.arange(n)
    head, tt = idx // c, idx % c
    same = head[:, None] == head[None, :]
    t = np.arange(c)
    tri = (t[:, None] >= t[None, :]).astype(np.float32)
    kind = np.zeros((n, n), np.int32)
    kind[same & (tt[:, None] == tt[None, :])] = 1
    kind[same & (tt[:, None] > tt[None, :])] = 2
    return jnp.asarray(tri, dtype=bf16), jnp.asarray(kind)


def _gdn_kernel(alog_ref, dtb_ref, cq_ref, ck_ref, cv_ref, cz_ref, sm_ref, cw_ref, gain_ref, tri_ref, kind_ref,
                o_ref, carry_ref, qkv_ref, s_ref):
    c = CHUNK
    tb = cq_ref.shape[0]
    hw = GDN_HEADS * GDN_DK

    @pl.when(pl.program_id(1) == 0)
    def _():
        carry_ref[...] = jnp.zeros_like(carry_ref)
        s_ref[...] = jnp.zeros_like(s_ref)

    row8 = lax.broadcasted_iota(jnp.int32, (8, LANES), 0)
    for part, src in enumerate((cq_ref, ck_ref, cv_ref)):
        for h in range(GDN_HEADS):
            col = part * hw + h * LANES
            x = src[:, h * LANES:(h + 1) * LANES]
            prev = carry_ref[:, col:col + LANES]
            acc = x * cw_ref[CONV_WIDTH - 1:CONV_WIDTH, col:col + LANES]
            for shift in range(1, CONV_WIDTH):
                xs = pltpu.roll(x, shift, 0)
                top = jnp.where(row8 < shift, pltpu.roll(prev, shift, 0), xs[0:8, :])
                xs = jnp.concatenate([top, xs[8:, :]], axis=0)
                acc = acc + xs * cw_ref[CONV_WIDTH - 1 - shift:CONV_WIDTH - shift, col:col + LANES]
            carry_ref[:, col:col + LANES] = x[tb - 8:tb, :]
            y = _silu(acc)
            if part < 2:
                y = y * lax.rsqrt(jnp.sum(y * y, axis=-1, keepdims=True) + EPS)
            if part == 0:
                y = y * (GDN_DK ** -0.5)
            qkv_ref[:, col:col + LANES] = y

    sm_all = sm_ref[...]
    g_all = -jnp.exp(alog_ref[...]) * jax.nn.softplus(sm_all + dtb_ref[...])
    beta_all = jax.nn.sigmoid(sm_all)
    tri = tri_ref[...]
    kind = kind_ref[...]

    def lane_to_rows(x, lane0):
        return jnp.concatenate(
            [jnp.broadcast_to(x[:, lane0 + h:lane0 + h + 1], (c, LANES)) for h in range(GDN_HEADS)], axis=0)

    subs = range(tb // c)
    rows = [slice(i * c, (i + 1) * c) for i in subs]

    def cumsum3(g_f):
        g_hi = g_f.astype(bf16)
        g_mid, g_lo = _split_hi_lo(g_f - g_hi.astype(f32))
        return _dot(tri, g_hi) + (_dot(tri, g_mid) + _dot(tri, g_lo))

    gcs = [lane_to_rows(cumsum3(g_all[r, :]), SM_CA) for r in rows]
    betas = [lane_to_rows(beta_all[r, :], SM_CB) for r in rows]
    decs = [jnp.exp(jnp.minimum(jnp.concatenate([gc, gc], axis=1) - gc.T[0:1, :], 0.0)) for gc in gcs]
    qsts = [_stack_heads(qkv_ref[r, 0:hw], GDN_DK, GDN_HEADS) for r in rows]
    ksts = [_stack_heads(qkv_ref[r, hw:2 * hw], GDN_DK, GDN_HEADS) for r in rows]
    vsts = [_stack_heads(qkv_ref[r, 2 * hw:3 * hw], GDN_DV, GDN_HEADS) for r in rows]
    egcs = [jnp.exp(gc) for gc in gcs]
    kbs = [kst * beta for kst, beta in zip(ksts, betas)]
    k16s = [kst.astype(bf16) for kst in ksts]
    m_negs = [jnp.where(kind == 2, -(_dot_nt(kb.astype(bf16), k16) * dec), 0.0)
              for kb, k16, dec in zip(kbs, k16s, decs)]
    attns = [jnp.where(kind >= 1, _dot_nt(qst.astype(bf16), k16) * dec, 0.0).astype(bf16)
             for qst, k16, dec in zip(qsts, k16s, decs)]
    rhss = [jnp.concatenate([vst * beta, kb * egc], axis=1)
            for vst, beta, kb, egc in zip(vsts, betas, kbs, egcs)]
    m_his = [m.astype(bf16) for m in m_negs]
    m_los = [(m - mh.astype(f32)).astype(bf16) for m, mh in zip(m_negs, m_his)]
    ps = m_his
    tinvs = m_negs
    for _ in range(4):
        ps = [_dot(p, p).astype(bf16) for p in ps]
        tinvs = [tinv + _dot(tinv.astype(bf16), p) + p.astype(f32) for tinv, p in zip(tinvs, ps)]
    t16s = [tinv.astype(bf16) for tinv in tinvs]
    x0s = [(rhs + _dot(t16, rhs.astype(bf16))).astype(bf16) for rhs, t16 in zip(rhss, t16s)]
    resids = [(rhs - x0.astype(f32)) + (_dot(mh, x0) + _dot(ml, x0))
              for rhs, x0, mh, ml in zip(rhss, x0s, m_his, m_los)]
    xss = [x0.astype(f32) + (resid + _dot(t16, resid.astype(bf16))) for x0, resid, t16 in zip(x0s, resids, t16s)]
    qgs = [qst * egc for qst, egc in zip(qsts, egcs)]

    for i in subs:
        r, gc, kst, xs, qg, attn = rows[i], gcs[i], ksts[i], xss[i], qgs[i], attns[i]
        vnew_parts, oq_parts = [], []
        for h in range(GDN_HEADS):
            hs = slice(h * c, (h + 1) * c)
            s16 = s_ref[h].astype(bf16)
            t = _dot(jnp.concatenate([xs[hs, GDN_DV:], qg[hs, :]], axis=0).astype(bf16), s16)
            vnew_parts.append(xs[hs, :GDN_DV] - t[:c])
            oq_parts.append(t[c:])
        vnew = jnp.concatenate(vnew_parts, axis=0)
        v16 = vnew.astype(bf16)
        o = jnp.concatenate(oq_parts, axis=0) + _dot(attn, v16)
        for h in range(GDN_HEADS):
            hs = slice(h * c, (h + 1) * c)
            g_last = gc[(h + 1) * c - 1:(h + 1) * c, :]
            kd = (kst[hs, :] * jnp.exp(g_last - gc[hs, :])).astype(bf16)
            s_ref[h] = s_ref[h] * jnp.exp(g_last) + _dot_tn(kd, v16[hs, :])
        ms = jnp.mean(o * o, axis=-1, keepdims=True)
        y = (o * lax.rsqrt(ms + EPS)) * gain_ref[...]
        y = y * _silu(_stack_heads(cz_ref[r, :], GDN_DV, GDN_HEADS))
        for h in range(GDN_HEADS):
            o_ref[r, h * GDN_DV:(h + 1) * GDN_DV] = y[h * c:(h + 1) * c, :].astype(o_ref.dtype)


def _gdn(proj, conv_w, a_log, dt_bias, gain, bsz, seq, tb=256):
    m = proj.shape[0]
    nb = seq // tb
    hw = GDN_HEADS * GDN_DK
    tri, kind = _gdn_constants()
    row = lambda b, i: b * nb + i
    const = lambda shape: pl.BlockSpec(shape, lambda b, i: (0,) * len(shape))
    a_log = jnp.zeros((1, LANES), f32).at[0, SM_CA:SM_CA + GDN_HEADS].set(a_log)
    dt_bias = jnp.zeros((1, LANES), f32).at[0, SM_CA:SM_CA + GDN_HEADS].set(dt_bias)
    return pl.pallas_call(
        _gdn_kernel,
        grid=(bsz, nb),
        in_specs=[
            const((1, LANES)), const((1, LANES)),
            pl.BlockSpec((tb, hw), lambda b, i: (row(b, i), COL_CQ // hw)),
            pl.BlockSpec((tb, hw), lambda b, i: (row(b, i), COL_CK // hw)),
            pl.BlockSpec((tb, hw), lambda b, i: (row(b, i), COL_CV // hw)),
            pl.BlockSpec((tb, hw), lambda b, i: (row(b, i), COL_CZ // hw)),
            pl.BlockSpec((tb, LANES), lambda b, i: (row(b, i), COL_SMALL // LANES)),
            const((CONV_WIDTH, 3 * hw)), const((1, GDN_DV)), const(tri.shape), const(kind.shape),
        ],
        out_specs=pl.BlockSpec((tb, hw), lambda b, i: (row(b, i), 0)),
        out_shape=jax.ShapeDtypeStruct((m, hw), bf16),
        scratch_shapes=[
            pltpu.VMEM((8, 3 * hw), f32),
            pltpu.VMEM((tb, 3 * hw), f32),
            pltpu.VMEM((GDN_HEADS, GDN_DK, GDN_DV), f32),
        ],
        compiler_params=_cp("parallel", "arbitrary"),
        name="gdn",
    )(a_log, dt_bias, proj, proj, proj, proj, proj, conv_w, gain.reshape(1, GDN_DV), tri, kind)


def _permute_w_in(w_in):
    depth, d, _ = w_in.shape
    pieces = [
        w_in[:, :, 1552:2576],
        w_in[:, :, 512:1536],
        w_in[:, :, 2832:4880],
        w_in[:, :, 0:512],
        w_in[:, :, 2576:2832],
        w_in[:, :, 1536:1552],
        w_in[:, :, 4880:4888],
    ]
    used = sum(p.shape[-1] for p in pieces)
    pieces.append(jnp.zeros((depth, d, PROJ_WIDTH - used), w_in.dtype))
    return jnp.concatenate(pieces, axis=-1).astype(bf16)


def kernel(x, c, positions, w_mod, b_mod, norm1_gain, norm2_gain, w_in, gla_w_gk, gla_b_gk, gla_norm_gain,
           swa_sinks, gdn_conv_w, gdn_a_log, gdn_dt_bias, gdn_norm_gain, w_out, ffn_w_gate, ffn_w_up, ffn_w_down,
           final_norm_gain):
    bsz, seq, d = x.shape
    depth = w_in.shape[0]
    mods = _modulation(c, w_mod, b_mod)
    cos_t, sin_t = _rope_tables(positions)
    w_in_p = _permute_w_in(w_in)
    w_out16 = w_out.astype(bf16)
    wg16, wu16, wd16 = ffn_w_gate.astype(bf16), ffn_w_up.astype(bf16), ffn_w_down.astype(bf16)
    x2 = x.reshape(bsz * seq, d)
    g1 = norm1_gain.reshape(depth, 1, d)
    g2 = norm2_gain.reshape(depth, 1, d)
    for l in range(depth):
        proj = _in_proj(x2, mods, g1, w_in_p, l, seq)
        o_a = _gla(proj, gla_w_gk[l], gla_b_gk[l], gla_norm_gain[l], bsz, seq)
        o_b = _swa(proj, cos_t, sin_t, swa_sinks[l], bsz, seq)
        o_c = _gdn(proj, gdn_conv_w[l], gdn_a_log[l], gdn_dt_bias[l], gdn_norm_gain[l], bsz, seq)
        x2 = _out_proj(x2, o_a, o_b, o_c, w_out16, mods, l, seq)
        x2 = _ffn(x2, mods, g2, final_norm_gain, wg16, wu16, wd16, l, seq, final_norm=(l == depth - 1))
    return x2.reshape(bsz, seq, d)
```

```python
import functools

import numpy as np
import jax
import jax.numpy as jnp
from jax import lax
from jax.experimental import pallas as pl
from jax.experimental.pallas import tpu as pltpu

f32 = jnp.float32
bf16 = jnp.bfloat16
_HI = lax.Precision.HIGHEST

D_MODEL = 2048
EPS = 1e-6
GLA_HEADS, GLA_DK, GLA_DV, GLA_RANK = 4, 64, 128, 16
GLA_NORMALIZER = 16.0
SWA_HEADS, SWA_KV_HEADS, SWA_HD, WINDOW = 16, 2, 64, 128
ROPE_THETA, ROPE_DIMS = 500000.0, 16
GDN_HEADS, GDN_DK, GDN_DV, CONV_WIDTH = 4, 128, 128, 4
D_FF = 5632
CHUNK = 64
LANES = 128

COL_BQ, COL_AV, COL_AG = 0, 1024, 1536
COL_CQ, COL_CK, COL_CV, COL_CZ = 2048, 2560, 3072, 3584
COL_AQ, COL_AK, COL_BKV, COL_SMALL = 4096, 4352, 4608, 4864
PROJ_WIDTH = 5120
SM_ALR, SM_CB, SM_CA = 0, 16, 20

VMEM_LIMIT = 56 * 1024 * 1024


def _cp(*sem):
    return pltpu.CompilerParams(dimension_semantics=sem, vmem_limit_bytes=VMEM_LIMIT)


def _dot(a, b):
    return jnp.dot(a, b, preferred_element_type=f32)


def _dot_nt(a, b):
    return lax.dot_general(a, b, (((1,), (1,)), ((), ())), preferred_element_type=f32)


def _dot_tn(a, b):
    return lax.dot_general(a, b, (((0,), (0,)), ((), ())), preferred_element_type=f32)


def _dot_hi(a, b):
    return jnp.dot(a, b, preferred_element_type=f32, precision=_HI)


def _silu(x):
    return x * jax.nn.sigmoid(x)


def _norm_mod_rows(x_ref, gain, scale, shift, h_ref, rows):
    n = x_ref.shape[0] // rows
    gs = gain * (1.0 + scale)

    def body(i, carry):
        sl = pl.ds(pl.multiple_of(i * rows, rows), rows)
        x = x_ref[sl, :]
        ms = jnp.mean(x * x, axis=-1, keepdims=True)
        h_ref[sl, :] = ((x * lax.rsqrt(ms + EPS)) * gs + shift).astype(h_ref.dtype)
        return carry

    lax.fori_loop(0, n, body, 0)


def _mod_kernel(c_ref, w_ref, b_ref, o_ref):
    c = c_ref[...]
    o_ref[0] = _dot(_silu(c).astype(bf16), w_ref[0].astype(bf16)) + b_ref[0]


def _modulation(c, w_mod, b_mod):
    depth, d, n = w_mod.shape
    bsz = c.shape[0]
    rows = 8
    c_pad = jnp.pad(c, ((0, rows - bsz), (0, 0)))
    tn = 1024
    out = pl.pallas_call(
        _mod_kernel,
        grid=(depth, n // tn),
        in_specs=[
            pl.BlockSpec((rows, d), lambda l, j: (0, 0)),
            pl.BlockSpec((1, d, tn), lambda l, j: (l, 0, j)),
            pl.BlockSpec((1, 1, tn), lambda l, j: (l, 0, j)),
        ],
        out_specs=pl.BlockSpec((1, rows, tn), lambda l, j: (l, 0, j)),
        out_shape=jax.ShapeDtypeStruct((depth, rows, n), f32),
        compiler_params=_cp("parallel", "parallel"),
        name="modulation",
    )(c_pad, w_mod, b_mod.reshape(depth, 1, n))
    return out[:, :bsz, :].reshape(depth, bsz, 6, d)


def _in_proj_kernel(x_ref, mod_ref, gain_ref, w_ref, o_ref, h_ref):
    @pl.when(pl.program_id(1) == 0)
    def _():
        _norm_mod_rows(x_ref, gain_ref[...], mod_ref[0, 1:2, :], mod_ref[0, 0:1, :], h_ref, 128)

    o_ref[...] = _dot(h_ref[...], w_ref[...])


def _in_proj(x2, mods, gains, w, l, seq, tm=1024, tn=1024):
    m, d = x2.shape
    n = w.shape[2]
    per_b = seq // tm
    return pl.pallas_call(
        _in_proj_kernel,
        grid=(m // tm, n // tn),
        in_specs=[
            pl.BlockSpec((tm, d), lambda i, j: (i, 0)),
            pl.BlockSpec((None, 1, 6, d), lambda i, j: (l, i // per_b, 0, 0)),
            pl.BlockSpec((None, 1, d), lambda i, j: (l, 0, 0)),
            pl.BlockSpec((None, d, tn), lambda i, j: (l, 0, j)),
        ],
        out_specs=pl.BlockSpec((tm, tn), lambda i, j: (i, j)),
        out_shape=jax.ShapeDtypeStruct((m, n), f32),
        scratch_shapes=[pltpu.VMEM((tm, d), bf16)],
        compiler_params=_cp("parallel", "arbitrary"),
        name="in_proj",
    )(x2, mods, gains, w)


def _out_proj_kernel(x_ref, oa_ref, ob_ref, oc_ref, w_ref, mod_ref, o_ref):
    ka, kb = oa_ref.shape[1], ob_ref.shape[1]
    acc = _dot(oa_ref[...], w_ref[0:ka, :])
    acc = acc + _dot(ob_ref[...], w_ref[ka:ka + kb, :])
    acc = acc + _dot(oc_ref[...], w_ref[ka + kb:, :])
    o_ref[...] = x_ref[...] + mod_ref[0, 2:3, :] * acc


def _out_proj(x2, oa, ob, oc, w_out, mods, l, seq, tm=512):
    m, d = x2.shape
    per_b = seq // tm
    ka, kb, kc = oa.shape[1], ob.shape[1], oc.shape[1]
    return pl.pallas_call(
        _out_proj_kernel,
        grid=(m // tm,),
        in_specs=[
            pl.BlockSpec((tm, d), lambda i: (i, 0)),
            pl.BlockSpec((tm, ka), lambda i: (i, 0)),
            pl.BlockSpec((tm, kb), lambda i: (i, 0)),
            pl.BlockSpec((tm, kc), lambda i: (i, 0)),
            pl.BlockSpec((None, ka + kb + kc, d), lambda i: (l, 0, 0)),
            pl.BlockSpec((None, 1, 6, d), lambda i: (l, i // per_b, 0, 0)),
        ],
        out_specs=pl.BlockSpec((tm, d), lambda i: (i, 0)),
        out_shape=jax.ShapeDtypeStruct((m, d), f32),
        compiler_params=_cp("parallel"),
        name="out_proj",
    )(x2, oa, ob, oc, w_out, mods)


def _ffn_kernel(x_ref, mod_ref, gain_ref, fgain_ref, wg_ref, wu_ref, wd_ref, o_ref, h_ref, acc_ref, *, final_norm):
    f = pl.program_id(1)

    @pl.when(f == 0)
    def _():
        _norm_mod_rows(x_ref, gain_ref[...], mod_ref[0, 4:5, :], mod_ref[0, 3:4, :], h_ref, 128)
        acc_ref[...] = jnp.zeros_like(acc_ref)

    h = h_ref[...]
    g = _dot(h, wg_ref[...])
    u = _dot(h, wu_ref[...])
    a = (_silu(g) * u).astype(bf16)
    acc_ref[...] += _dot(a, wd_ref[...])

    @pl.when(f == pl.num_programs(1) - 1)
    def _():
        y = x_ref[...] + mod_ref[0, 5:6, :] * acc_ref[...]
        if final_norm:
            ms = jnp.mean(y * y, axis=-1, keepdims=True)
            y = (y * lax.rsqrt(ms + EPS)) * fgain_ref[...]
        o_ref[...] = y


def _ffn(x2, mods, gains, fgain, wg, wu, wd, l, seq, final_norm, tm=512, tf=512):
    m, d = x2.shape
    dff = wg.shape[2]
    per_b = seq // tm
    return pl.pallas_call(
        functools.partial(_ffn_kernel, final_norm=final_norm),
        grid=(m // tm, dff // tf),
        in_specs=[
            pl.BlockSpec((tm, d), lambda i, j: (i, 0)),
            pl.BlockSpec((None, 1, 6, d), lambda i, j: (l, i // per_b, 0, 0)),
            pl.BlockSpec((None, 1, d), lambda i, j: (l, 0, 0)),
            pl.BlockSpec((1, d), lambda i, j: (0, 0)),
            pl.BlockSpec((None, d, tf), lambda i, j: (l, 0, j)),
            pl.BlockSpec((None, d, tf), lambda i, j: (l, 0, j)),
            pl.BlockSpec((None, tf, d), lambda i, j: (l, j, 0)),
        ],
        out_specs=pl.BlockSpec((tm, d), lambda i, j: (i, 0)),
        out_shape=jax.ShapeDtypeStruct((m, d), f32),
        scratch_shapes=[pltpu.VMEM((tm, d), bf16), pltpu.VMEM((tm, d), f32)],
        compiler_params=_cp("parallel", "arbitrary"),
        name="ffn",
    )(x2, mods, gains, fgain.reshape(1, d), wg, wu, wd)


def _rope_kernel(pos_ref, invf_ref, sgn_ref, cos_ref, sin_ref):
    ang = pos_ref[...].astype(f32) * invf_ref[...]
    cos_ref[...] = jnp.cos(ang)
    sin_ref[...] = jnp.sin(ang) * sgn_ref[...]


def _rope_tables(positions):
    m = positions.size
    half = ROPE_DIMS // 2
    inv_freq = ROPE_THETA ** (-jnp.arange(half, dtype=f32) / half)
    dim = np.arange(LANES) % SWA_HD
    invf = jnp.where(jnp.asarray(dim < ROPE_DIMS), inv_freq[jnp.asarray(dim % half)], 0.0).reshape(1, LANES)
    sgn = jnp.asarray(np.where(dim < half, -1.0, 1.0), dtype=f32).reshape(1, LANES)
    pos_b = jnp.broadcast_to(positions.reshape(m, 1), (m, LANES))
    tr = 2048
    return pl.pallas_call(
        _rope_kernel,
        grid=(m // tr,),
        in_specs=[
            pl.BlockSpec((tr, LANES), lambda i: (i, 0)),
            pl.BlockSpec((1, LANES), lambda i: (0, 0)),
            pl.BlockSpec((1, LANES), lambda i: (0, 0)),
        ],
        out_specs=[pl.BlockSpec((tr, LANES), lambda i: (i, 0))] * 2,
        out_shape=[jax.ShapeDtypeStruct((m, LANES), f32)] * 2,
        compiler_params=_cp("parallel"),
        name="rope_tables",
    )(pos_b, invf, sgn)


def _swa_kernel(sink_ref, q_ref, kv_ref, cos_ref, sin_ref, o_ref, kvprev_ref):
    blk = pl.program_id(1)
    w = WINDOW

    @pl.when(blk == 0)
    def _():
        kvprev_ref[...] = jnp.zeros_like(kvprev_ref)

    cos = cos_ref[...]
    sin = sin_ref[...]
    lane = lax.broadcasted_iota(jnp.int32, (1, LANES), 1)
    low_half = (lane % SWA_HD) < (ROPE_DIMS // 2)
    first_head = lane < SWA_HD

    def rot(x):
        swapped = jnp.where(low_half, pltpu.roll(x, LANES - ROPE_DIMS // 2, 1), pltpu.roll(x, ROPE_DIMS // 2, 1))
        return x * cos + swapped * sin

    kv = kv_ref[...]
    kcur = rot(kv[:, :LANES])
    vcur = kv[:, LANES:]
    kk = jnp.concatenate([kvprev_ref[:, :LANES], kcur], axis=0)
    vv = jnp.concatenate([kvprev_ref[:, LANES:], vcur], axis=0)
    kk_sw = pltpu.roll(kk, SWA_HD, 1)
    vv_sw = pltpu.roll(vv, SWA_HD, 1)
    kdup = [jnp.where(first_head, kk, kk_sw).astype(bf16), jnp.where(first_head, kk_sw, kk).astype(bf16)]
    vlo = [jnp.where(first_head, vv, 0.0).astype(bf16), jnp.where(first_head, vv_sw, 0.0).astype(bf16)]
    vhi = [jnp.where(first_head, 0.0, vv_sw).astype(bf16), jnp.where(first_head, 0.0, vv).astype(bf16)]

    qi = lax.broadcasted_iota(jnp.int32, (w, 2 * w), 0)
    kj = lax.broadcasted_iota(jnp.int32, (w, 2 * w), 1)
    dist = qi + w - kj
    valid = (dist >= 0) & (dist < w) & ((kj >= w) | (blk > 0))

    group = SWA_HEADS // SWA_KV_HEADS
    batch = 4
    for t0 in range(0, SWA_HEADS // 2, batch):
        tiles = range(t0, t0 + batch)
        heads = [2 * t + half for t in tiles for half in range(2)]
        qts = [rot(q_ref[:, t * LANES:(t + 1) * LANES]) * (SWA_HD ** -0.5) for t in tiles]
        qhs = [jnp.where(first_head if h % 2 == 0 else ~first_head, qts[h // 2 - t0], 0.0).astype(bf16) for h in heads]
        ss = [jnp.where(valid, _dot_nt(qh, kdup[h // group]), -jnp.inf) for qh, h in zip(qhs, heads)]
        mxs = [jnp.maximum(jnp.max(s, axis=-1, keepdims=True), sink_ref[h]) for s, h in zip(ss, heads)]
        ps = [jnp.exp(s - mx) for s, mx in zip(ss, mxs)]
        dens = [jnp.sum(p, axis=-1, keepdims=True) + jnp.exp(sink_ref[h] - mx) for p, mx, h in zip(ps, mxs, heads)]
        os_ = [_dot(p.astype(bf16), (vlo if h % 2 == 0 else vhi)[h // group]) / den for p, den, h in zip(ps, dens, heads)]
        for t in tiles:
            k = 2 * (t - t0)
            o_ref[:, t * LANES:(t + 1) * LANES] = (os_[k] + os_[k + 1]).astype(o_ref.dtype)

    kvprev_ref[:, :LANES] = kcur
    kvprev_ref[:, LANES:] = vcur


def _swa(proj, cos_t, sin_t, sinks, bsz, seq):
    m = proj.shape[0]
    nb = seq // WINDOW
    qw = SWA_HEADS * SWA_HD
    kvw = 2 * SWA_KV_HEADS * SWA_HD
    row = lambda b, i: b * nb + i
    return pl.pallas_call(
        _swa_kernel,
        grid=(bsz, nb),
        in_specs=[
            pl.BlockSpec(memory_space=pltpu.SMEM),
            pl.BlockSpec((WINDOW, qw), lambda b, i: (row(b, i), COL_BQ // qw)),
            pl.BlockSpec((WINDOW, kvw), lambda b, i: (row(b, i), COL_BKV // kvw)),
            pl.BlockSpec((WINDOW, LANES), lambda b, i: (row(b, i), 0)),
            pl.BlockSpec((WINDOW, LANES), lambda b, i: (row(b, i), 0)),
        ],
        out_specs=pl.BlockSpec((WINDOW, qw), lambda b, i: (row(b, i), 0)),
        out_shape=jax.ShapeDtypeStruct((m, qw), bf16),
        scratch_shapes=[pltpu.VMEM((WINDOW, kvw), f32)],
        compiler_params=_cp("parallel", "arbitrary"),
        name="swa",
    )(sinks, proj, proj, cos_t, sin_t)


_GLA_LEVELS = (32, 16, 8, 4, 2, 1)


def _gla_constants():
    c = CHUNK
    t = np.arange(c)
    tri = (t[:, None] >= t[None, :]).astype(np.float32)
    sel = np.zeros((len(_GLA_LEVELS) * c, c), np.float32)
    for l, mhalf in enumerate(_GLA_LEVELS):
        mid = (t // (2 * mhalf)) * 2 * mhalf + mhalf - 1
        sel[l * c + t, mid] = 1.0
    cum_sel = np.concatenate([tri, sel @ tri], axis=0)
    n = GLA_HEADS * c
    idx = np.arange(n)
    head, tt = idx // c, idx % c
    same = head[:, None] == head[None, :]
    level = np.full((n, n), -1, np.int32)
    for l, mhalf in enumerate(_GLA_LEVELS):
        blk = (tt[:, None] // (2 * mhalf)) == (tt[None, :] // (2 * mhalf))
        msk = same & blk & ((tt[:, None] % (2 * mhalf)) >= mhalf) & ((tt[None, :] % (2 * mhalf)) < mhalf)
        level[msk] = l
    level[same & (tt[:, None] == tt[None, :])] = len(_GLA_LEVELS)
    return jnp.asarray(cum_sel, dtype=bf16), jnp.asarray(level)


def _stack_heads(x, width, heads):
    return jnp.concatenate([x[:, h * width:(h + 1) * width] for h in range(heads)], axis=0)


def _split_hi_lo(x):
    hi = x.astype(bf16)
    return hi, (x - hi.astype(f32)).astype(bf16)


def _gla_kernel(q_ref, k_ref, v_ref, g_ref, sm_ref, wgk_ref, bgk_ref, gain_ref, cumsel_ref, lv_ref, o_ref, st_ref):
    c = CHUNK

    @pl.when(pl.program_id(1) == 0)
    def _():
        st_ref[...] = jnp.zeros_like(st_ref)

    lane = lax.broadcasted_iota(jnp.int32, (1, GLA_HEADS * GLA_DK), 1)
    head_mask = [(lane // GLA_DK) == h for h in range(GLA_HEADS)]

    def per_head_rows(x):
        x16 = x.astype(bf16)
        return jnp.concatenate([jnp.where(head_mask[h], x16, jnp.zeros_like(x16)) for h in range(GLA_HEADS)], axis=0)

    log_a_all = jax.nn.log_sigmoid(_dot(sm_ref[...].astype(bf16), wgk_ref[...]) + bgk_ref[...]) * (1.0 / GLA_NORMALIZER)
    cum_sel = cumsel_ref[...]
    lv = lv_ref[...]
    subs = range(q_ref.shape[0] // c)
    rows = [slice(i * c, (i + 1) * c) for i in subs]
    pair = 2 * c
    qs = [q_ref[r, :] * (GLA_DK ** -0.5) for r in rows]
    ks = [k_ref[r, :] for r in rows]
    bbs = []
    for r in rows:
        la_hi, la_lo = _split_hi_lo(log_a_all[r, :])
        bbs.append(_dot(cum_sel, la_hi) + _dot(cum_sel, la_lo))
    bs = [bb[:c, :] for bb in bbs]
    accs = [[jnp.zeros((c, pair), f32) for _ in range(GLA_HEADS)] for _ in subs]
    for l in range(len(_GLA_LEVELS) + 1):
        for i in subs:
            q, k, b = qs[i], ks[i], bs[i]
            if l < len(_GLA_LEVELS):
                bm = bbs[i][(l + 1) * c:(l + 2) * c, :]
                qe = q * jnp.exp(jnp.minimum(b - bm, 0.0))
                ke = k * jnp.exp(jnp.minimum(bm - b, 0.0))
            else:
                qe, ke = q, k
            ke16 = ke.astype(bf16)
            res = _dot_nt(per_head_rows(qe), jnp.concatenate([ke16] * GLA_HEADS, axis=0))
            for h in range(GLA_HEADS):
                hr = slice(h * c, (h + 1) * c)
                hc = slice((h // 2) * pair, (h // 2 + 1) * pair)
                accs[i][h] = jnp.where(lv[hr, hc] == l, res[hr, hc], accs[i][h])
    vsts = [_stack_heads(v_ref[r, :], GLA_DV, GLA_HEADS).astype(bf16) for r in rows]
    o_intra = []
    for i in subs:
        parts = []
        for hp in range(GLA_HEADS // 2):
            a_pair = jnp.concatenate([accs[i][2 * hp], accs[i][2 * hp + 1]], axis=0).astype(bf16)
            parts.append(_dot(a_pair, vsts[i][hp * pair:(hp + 1) * pair, :]))
        o_intra.append(jnp.concatenate(parts, axis=0))
    qgs = [per_head_rows(q * jnp.exp(b)) for q, b in zip(qs, bs)]
    kds = [per_head_rows(k * jnp.exp(b[c - 1:c, :] - b)) for k, b in zip(ks, bs)]

    for i in subs:
        r, vst = rows[i], vsts[i]
        st = st_ref[...]
        o = o_intra[i] + _dot_nt(qgs[i], st.astype(bf16))
        st_ref[...] = st * jnp.exp(bs[i][c - 1:c, :]) + _dot_tn(vst, kds[i])
        ms = jnp.mean(o * o, axis=-1, keepdims=True)
        y = (o * lax.rsqrt(ms + EPS)) * gain_ref[...]
        y = y * _silu(_stack_heads(g_ref[r, :], GLA_DV, GLA_HEADS))
        for h in range(GLA_HEADS):
            o_ref[r, h * GLA_DV:(h + 1) * GLA_DV] = y[h * c:(h + 1) * c, :].astype(o_ref.dtype)


def _gla(proj, w_gk, b_gk, gain, bsz, seq, tb=256):
    m = proj.shape[0]
    nb = seq // tb
    qk_w = GLA_HEADS * GLA_DK
    v_w = GLA_HEADS * GLA_DV
    cum_sel, level = _gla_constants()
    wgk_pad = jnp.zeros((LANES, qk_w), f32).at[SM_ALR:SM_ALR + GLA_RANK].set(w_gk).astype(bf16)
    row = lambda b, i: b * nb + i
    const = lambda shape: pl.BlockSpec(shape, lambda b, i: (0,) * len(shape))
    return pl.pallas_call(
        _gla_kernel,
        grid=(bsz, nb),
        in_specs=[
            pl.BlockSpec((tb, qk_w), lambda b, i: (row(b, i), COL_AQ // qk_w)),
            pl.BlockSpec((tb, qk_w), lambda b, i: (row(b, i), COL_AK // qk_w)),
            pl.BlockSpec((tb, v_w), lambda b, i: (row(b, i), COL_AV // v_w)),
            pl.BlockSpec((tb, v_w), lambda b, i: (row(b, i), COL_AG // v_w)),
            pl.BlockSpec((tb, LANES), lambda b, i: (row(b, i), COL_SMALL // LANES)),
            const((LANES, qk_w)), const((1, qk_w)), const((1, GLA_DV)),
            const(cum_sel.shape), const(level.shape),
        ],
        out_specs=pl.BlockSpec((tb, v_w), lambda b, i: (row(b, i), 0)),
        out_shape=jax.ShapeDtypeStruct((m, v_w), bf16),
        scratch_shapes=[pltpu.VMEM((GLA_DV, qk_w), f32)],
        compiler_params=_cp("parallel", "arbitrary"),
        name="gla",
    )(proj, proj, proj, proj, proj, wgk_pad, b_gk.reshape(1, qk_w), gain.reshape(1, GLA_DV), cum_sel, level)


_GDN_GROUP = 2


def _gdn_constants():
    c = CHUNK
    n = _GDN_GROUP * c
    idx = np.arange(n)
    head, tt = idx // c, idx % c
    same = head[:, None] == head[None, :]
    t = np.arange(c)
    tri = (t[:, None] >= t[None, :]).astype(np.float32)
    kind = np.zeros((n, n), np.int32)
    kind[same & (tt[:, None] == tt[None, :])] = 1
    kind[same & (tt[:, None] > tt[None, :])] = 2
    return jnp.asarray(tri, dtype=bf16), jnp.asarray(kind)


def _gdn_kernel(alog_ref, dtb_ref, cq_ref, ck_ref, cv_ref, cz_ref, sm_ref, cw_ref, gain_ref, tri_ref, kind_ref,
                o_ref, carry_ref, qkv_ref, s_ref):
    c = CHUNK
    tb = cq_ref.shape[0]
    hw = GDN_HEADS * GDN_DK

    @pl.when(pl.program_id(1) == 0)
    def _():
        carry_ref[...] = jnp.zeros_like(carry_ref)
        s_ref[...] = jnp.zeros_like(s_ref)

    row8 = lax.broadcasted_iota(jnp.int32, (8, LANES), 0)
    for part, src in enumerate((cq_ref, ck_ref, cv_ref)):
        for h in range(GDN_HEADS):
            col = part * hw + h * LANES
            x = src[:, h * LANES:(h + 1) * LANES]
            prev = carry_ref[:, col:col + LANES]
            acc = x * cw_ref[CONV_WIDTH - 1:CONV_WIDTH, col:col + LANES]
            for shift in range(1, CONV_WIDTH):
                xs = pltpu.roll(x, shift, 0)
                top = jnp.where(row8 < shift, pltpu.roll(prev, shift, 0), xs[0:8, :])
                xs = jnp.concatenate([top, xs[8:, :]], axis=0)
                acc = acc + xs * cw_ref[CONV_WIDTH - 1 - shift:CONV_WIDTH - shift, col:col + LANES]
            carry_ref[:, col:col + LANES] = x[tb - 8:tb, :]
            y = _silu(acc)
            if part < 2:
                y = y * lax.rsqrt(jnp.sum(y * y, axis=-1, keepdims=True) + EPS)
            if part == 0:
                y = y * (GDN_DK ** -0.5)
            qkv_ref[:, col:col + LANES] = y

    sm_all = sm_ref[...]
    g_all = -jnp.exp(alog_ref[...]) * jax.nn.softplus(sm_all + dtb_ref[...])
    beta_all = jax.nn.sigmoid(sm_all)
    tri = tri_ref[...]
    kind = kind_ref[...]

    def lane_to_rows(x, lane0):
        return jnp.concatenate(
            [jnp.broadcast_to(x[:, lane0 + h:lane0 + h + 1], (c, LANES)) for h in range(_GDN_GROUP)], axis=0)

    subs = range(tb // c)
    groups = range(GDN_HEADS // _GDN_GROUP)
    gw = _GDN_GROUP * GDN_DK
    units = [(i, gp) for i in subs for gp in groups]
    rows = [slice(i * c, (i + 1) * c) for i in subs]

    def cumsum3(g_f):
        g_hi = g_f.astype(bf16)
        g_mid, g_lo = _split_hi_lo(g_f - g_hi.astype(f32))
        return _dot(tri, g_hi) + (_dot(tri, g_mid) + _dot(tri, g_lo))

    def pair(i, gp, part):
        lo = part * hw + gp * gw
        return _stack_heads(qkv_ref[rows[i], lo:lo + gw], GDN_DK, _GDN_GROUP)

    gc_lanes = [cumsum3(g_all[r, :]) for r in rows]
    gcs = [lane_to_rows(gc_lanes[i], SM_CA + gp * _GDN_GROUP) for i, gp in units]
    betas = [lane_to_rows(beta_all[rows[i], :], SM_CB + gp * _GDN_GROUP) for i, gp in units]
    decs = [jnp.exp(jnp.minimum(gc - gc.T[0:1, :], 0.0)) for gc in gcs]
    qsts = [pair(i, gp, 0) for i, gp in units]
    ksts = [pair(i, gp, 1) for i, gp in units]
    vsts = [pair(i, gp, 2) for i, gp in units]
    egcs = [jnp.exp(gc) for gc in gcs]
    kbs = [kst * beta for kst, beta in zip(ksts, betas)]
    k16s = [kst.astype(bf16) for kst in ksts]
    m_negs = [jnp.where(kind == 2, -(_dot_nt(kb.astype(bf16), k16) * dec), 0.0)
              for kb, k16, dec in zip(kbs, k16s, decs)]
    attns = [jnp.where(kind >= 1, _dot_nt(qst.astype(bf16), k16) * dec, 0.0).astype(bf16)
             for qst, k16, dec in zip(qsts, k16s, decs)]
    rhss = [jnp.concatenate([vst * beta, kb * egc], axis=1)
            for vst, beta, kb, egc in zip(vsts, betas, kbs, egcs)]
    m_his = [m.astype(bf16) for m in m_negs]
    m_los = [(m - mh.astype(f32)).astype(bf16) for m, mh in zip(m_negs, m_his)]
    ps = m_his
    tinvs = m_negs
    for _ in range(4):
        ps = [_dot(p, p).astype(bf16) for p in ps]
        tinvs = [tinv + _dot(tinv.astype(bf16), p) + p.astype(f32) for tinv, p in zip(tinvs, ps)]
    t16s = [tinv.astype(bf16) for tinv in tinvs]
    x0s = [(rhs + _dot(t16, rhs.astype(bf16))).astype(bf16) for rhs, t16 in zip(rhss, t16s)]
    resids = [(rhs - x0.astype(f32)) + (_dot(mh, x0) + _dot(ml, x0))
              for rhs, x0, mh, ml in zip(rhss, x0s, m_his, m_los)]
    xss = [x0.astype(f32) + (resid + _dot(t16, resid.astype(bf16))) for x0, resid, t16 in zip(x0s, resids, t16s)]
    qgs = [qst * egc for qst, egc in zip(qsts, egcs)]

    for u, (i, gp) in enumerate(units):
        r, gc, kst, xs, qg, attn = rows[i], gcs[u], ksts[u], xss[u], qgs[u], attns[u]
        heads = [gp * _GDN_GROUP + hl for hl in range(_GDN_GROUP)]
        vnew_parts, oq_parts = [], []
        for hl, h in enumerate(heads):
            hs = slice(hl * c, (hl + 1) * c)
            s16 = s_ref[h].astype(bf16)
            t = _dot(jnp.concatenate([xs[hs, GDN_DV:], qg[hs, :]], axis=0).astype(bf16), s16)
            vnew_parts.append(xs[hs, :GDN_DV] - t[:c])
            oq_parts.append(t[c:])
        vnew = jnp.concatenate(vnew_parts, axis=0)
        v16 = vnew.astype(bf16)
        o = jnp.concatenate(oq_parts, axis=0) + _dot(attn, v16)
        for hl, h in enumerate(heads):
            hs = slice(hl * c, (hl + 1) * c)
            g_last = gc[(hl + 1) * c - 1:(hl + 1) * c, :]
            kd = (kst[hs, :] * jnp.exp(g_last - gc[hs, :])).astype(bf16)
            s_ref[h] = s_ref[h] * jnp.exp(g_last) + _dot_tn(kd, v16[hs, :])
        ms = jnp.mean(o * o, axis=-1, keepdims=True)
        y = (o * lax.rsqrt(ms + EPS)) * gain_ref[...]
        y = y * _silu(_stack_heads(cz_ref[r, gp * gw:(gp + 1) * gw], GDN_DV, _GDN_GROUP))
        for hl, h in enumerate(heads):
            o_ref[r, h * GDN_DV:(h + 1) * GDN_DV] = y[hl * c:(hl + 1) * c, :].astype(o_ref.dtype)


def _gdn(proj, conv_w, a_log, dt_bias, gain, bsz, seq, tb=256):
    m = proj.shape[0]
    nb = seq // tb
    hw = GDN_HEADS * GDN_DK
    tri, kind = _gdn_constants()
    row = lambda b, i: b * nb + i
    const = lambda shape: pl.BlockSpec(shape, lambda b, i: (0,) * len(shape))
    a_log = jnp.zeros((1, LANES), f32).at[0, SM_CA:SM_CA + GDN_HEADS].set(a_log)
    dt_bias = jnp.zeros((1, LANES), f32).at[0, SM_CA:SM_CA + GDN_HEADS].set(dt_bias)
    return pl.pallas_call(
        _gdn_kernel,
        grid=(bsz, nb),
        in_specs=[
            const((1, LANES)), const((1, LANES)),
            pl.BlockSpec((tb, hw), lambda b, i: (row(b, i), COL_CQ // hw)),
            pl.BlockSpec((tb, hw), lambda b, i: (row(b, i), COL_CK // hw)),
            pl.BlockSpec((tb, hw), lambda b, i: (row(b, i), COL_CV // hw)),
            pl.BlockSpec((tb, hw), lambda b, i: (row(b, i), COL_CZ // hw)),
            pl.BlockSpec((tb, LANES), lambda b, i: (row(b, i), COL_SMALL // LANES)),
            const((CONV_WIDTH, 3 * hw)), const((1, GDN_DV)), const(tri.shape), const(kind.shape),
        ],
        out_specs=pl.BlockSpec((tb, hw), lambda b, i: (row(b, i), 0)),
        out_shape=jax.ShapeDtypeStruct((m, hw), bf16),
        scratch_shapes=[
            pltpu.VMEM((8, 3 * hw), f32),
            pltpu.VMEM((tb, 3 * hw), f32),
            pltpu.VMEM((GDN_HEADS, GDN_DK, GDN_DV), f32),
        ],
        compiler_params=_cp("parallel", "arbitrary"),
        name="gdn",
    )(a_log, dt_bias, proj, proj, proj, proj, proj, conv_w, gain.reshape(1, GDN_DV), tri, kind)


def _permute_w_in(w_in):
    depth, d, _ = w_in.shape
    pieces = [
        w_in[:, :, 1552:2576],
        w_in[:, :, 512:1536],
        w_in[:, :, 2832:4880],
        w_in[:, :, 0:512],
        w_in[:, :, 2576:2832],
        w_in[:, :, 1536:1552],
        w_in[:, :, 4880:4888],
    ]
    used = sum(p.shape[-1] for p in pieces)
    pieces.append(jnp.zeros((depth, d, PROJ_WIDTH - used), w_in.dtype))
    return jnp.concatenate(pieces, axis=-1).astype(bf16)


def kernel(x, c, positions, w_mod, b_mod, norm1_gain, norm2_gain, w_in, gla_w_gk, gla_b_gk, gla_norm_gain,
           swa_sinks, gdn_conv_w, gdn_a_log, gdn_dt_bias, gdn_norm_gain, w_out, ffn_w_gate, ffn_w_up, ffn_w_down,
           final_norm_gain):
    bsz, seq, d = x.shape
    depth = w_in.shape[0]
    mods = _modulation(c, w_mod, b_mod)
    cos_t, sin_t = _rope_tables(positions)
    w_in_p = _permute_w_in(w_in)
    w_out16 = w_out.astype(bf16)
    wg16, wu16, wd16 = ffn_w_gate.astype(bf16), ffn_w_up.astype(bf16), ffn_w_down.astype(bf16)
    x2 = x.reshape(bsz * seq, d)
    g1 = norm1_gain.reshape(depth, 1, d)
    g2 = norm2_gain.reshape(depth, 1, d)
    for l in range(depth):
        proj = _in_proj(x2, mods, g1, w_in_p, l, seq)
        o_a = _gla(proj, gla_w_gk[l], gla_b_gk[l], gla_norm_gain[l], bsz, seq)
        o_b = _swa(proj, cos_t, sin_t, swa_sinks[l], bsz, seq)
        o_c = _gdn(proj, gdn_conv_w[l], gdn_a_log[l], gdn_dt_bias[l], gdn_norm_gain[l], bsz, seq)
        x2 = _out_proj(x2, o_a, o_b, o_c, w_out16, mods, l, seq)
        x2 = _ffn(x2, mods, g2, final_norm_gain, wg16, wu16, wd16, l, seq, final_norm=(l == depth - 1))
    return x2.reshape(bsz, seq, d)
```

```python
import functools

import numpy as np
import jax
import jax.numpy as jnp
from jax import lax
from jax.experimental import pallas as pl
from jax.experimental.pallas import tpu as pltpu

f32 = jnp.float32
bf16 = jnp.bfloat16
_HI = lax.Precision.HIGHEST

D_MODEL = 2048
EPS = 1e-6
GLA_HEADS, GLA_DK, GLA_DV, GLA_RANK = 4, 64, 128, 16
GLA_NORMALIZER = 16.0
SWA_HEADS, SWA_KV_HEADS, SWA_HD, WINDOW = 16, 2, 64, 128
ROPE_THETA, ROPE_DIMS = 500000.0, 16
GDN_HEADS, GDN_DK, GDN_DV, CONV_WIDTH = 4, 128, 128, 4
D_FF = 5632
CHUNK = 64
LANES = 128

COL_BQ, COL_AV, COL_AG = 0, 1024, 1536
COL_CQ, COL_CK, COL_CV, COL_CZ = 2048, 2560, 3072, 3584
COL_AQ, COL_AK, COL_BKV, COL_SMALL = 4096, 4352, 4608, 4864
PROJ_WIDTH = 5120
SM_ALR, SM_CB, SM_CA = 0, 16, 20

VMEM_LIMIT = 56 * 1024 * 1024


def _cp(*sem):
    return pltpu.CompilerParams(dimension_semantics=sem, vmem_limit_bytes=VMEM_LIMIT)


def _dot(a, b):
    return jnp.dot(a, b, preferred_element_type=f32)


def _dot_nt(a, b):
    return lax.dot_general(a, b, (((1,), (1,)), ((), ())), preferred_element_type=f32)


def _dot_tn(a, b):
    return lax.dot_general(a, b, (((0,), (0,)), ((), ())), preferred_element_type=f32)


def _dot_hi(a, b):
    return jnp.dot(a, b, preferred_element_type=f32, precision=_HI)


def _silu(x):
    return x * jax.nn.sigmoid(x)


def _norm_mod_rows(x_ref, gain, scale, shift, h_ref, rows):
    n = x_ref.shape[0] // rows
    gs = gain * (1.0 + scale)

    def body(i, carry):
        sl = pl.ds(pl.multiple_of(i * rows, rows), rows)
        x = x_ref[sl, :]
        ms = jnp.mean(x * x, axis=-1, keepdims=True)
        h_ref[sl, :] = ((x * lax.rsqrt(ms + EPS)) * gs + shift).astype(h_ref.dtype)
        return carry

    lax.fori_loop(0, n, body, 0)


def _mod_kernel(c_ref, w_ref, b_ref, o_ref):
    c = c_ref[...]
    o_ref[0] = _dot(_silu(c).astype(bf16), w_ref[0].astype(bf16)) + b_ref[0]


def _modulation(c, w_mod, b_mod):
    depth, d, n = w_mod.shape
    bsz = c.shape[0]
    rows = 8
    c_pad = jnp.pad(c, ((0, rows - bsz), (0, 0)))
    tn = 1024
    out = pl.pallas_call(
        _mod_kernel,
        grid=(depth, n // tn),
        in_specs=[
            pl.BlockSpec((rows, d), lambda l, j: (0, 0)),
            pl.BlockSpec((1, d, tn), lambda l, j: (l, 0, j)),
            pl.BlockSpec((1, 1, tn), lambda l, j: (l, 0, j)),
        ],
        out_specs=pl.BlockSpec((1, rows, tn), lambda l, j: (l, 0, j)),
        out_shape=jax.ShapeDtypeStruct((depth, rows, n), f32),
        compiler_params=_cp("parallel", "parallel"),
        name="modulation",
    )(c_pad, w_mod, b_mod.reshape(depth, 1, n))
    return out[:, :bsz, :].reshape(depth, bsz, 6, d)


def _in_proj_kernel(x_ref, mod_ref, gain_ref, w_ref, o_ref, h_ref):
    @pl.when(pl.program_id(1) == 0)
    def _():
        _norm_mod_rows(x_ref, gain_ref[...], mod_ref[0, 1:2, :], mod_ref[0, 0:1, :], h_ref, 128)

    o_ref[...] = _dot(h_ref[...], w_ref[...])


def _in_proj(x2, mods, gains, w, l, seq, tm=1024, tn=1024):
    m, d = x2.shape
    n = w.shape[2]
    per_b = seq // tm
    return pl.pallas_call(
        _in_proj_kernel,
        grid=(m // tm, n // tn),
        in_specs=[
            pl.BlockSpec((tm, d), lambda i, j: (i, 0)),
            pl.BlockSpec((None, 1, 6, d), lambda i, j: (l, i // per_b, 0, 0)),
            pl.BlockSpec((None, 1, d), lambda i, j: (l, 0, 0)),
            pl.BlockSpec((None, d, tn), lambda i, j: (l, 0, j)),
        ],
        out_specs=pl.BlockSpec((tm, tn), lambda i, j: (i, j)),
        out_shape=jax.ShapeDtypeStruct((m, n), f32),
        scratch_shapes=[pltpu.VMEM((tm, d), bf16)],
        compiler_params=_cp("parallel", "arbitrary"),
        name="in_proj",
    )(x2, mods, gains, w)


def _out_proj_kernel(x_ref, oa_ref, ob_ref, oc_ref, w_ref, mod_ref, o_ref):
    ka, kb = oa_ref.shape[1], ob_ref.shape[1]
    acc = _dot(oa_ref[...], w_ref[0:ka, :])
    acc = acc + _dot(ob_ref[...], w_ref[ka:ka + kb, :])
    acc = acc + _dot(oc_ref[...], w_ref[ka + kb:, :])
    o_ref[...] = x_ref[...] + mod_ref[0, 2:3, :] * acc


def _out_proj(x2, oa, ob, oc, w_out, mods, l, seq, tm=512):
    m, d = x2.shape
    per_b = seq // tm
    ka, kb, kc = oa.shape[1], ob.shape[1], oc.shape[1]
    return pl.pallas_call(
        _out_proj_kernel,
        grid=(m // tm,),
        in_specs=[
            pl.BlockSpec((tm, d), lambda i: (i, 0)),
            pl.BlockSpec((tm, ka), lambda i: (i, 0)),
            pl.BlockSpec((tm, kb), lambda i: (i, 0)),
            pl.BlockSpec((tm, kc), lambda i: (i, 0)),
            pl.BlockSpec((None, ka + kb + kc, d), lambda i: (l, 0, 0)),
            pl.BlockSpec((None, 1, 6, d), lambda i: (l, i // per_b, 0, 0)),
        ],
        out_specs=pl.BlockSpec((tm, d), lambda i: (i, 0)),
        out_shape=jax.ShapeDtypeStruct((m, d), f32),
        compiler_params=_cp("parallel"),
        name="out_proj",
    )(x2, oa, ob, oc, w_out, mods)


def _ffn_kernel(x_ref, mod_ref, gain_ref, fgain_ref, wg_ref, wu_ref, wd_ref, o_ref, h_ref, *, final_norm):
    f = pl.program_id(1)
    rows = 128

    @pl.when(f == 0)
    def _():
        _norm_mod_rows(x_ref, gain_ref[...], mod_ref[0, 4:5, :], mod_ref[0, 3:4, :], h_ref, rows)
        o_ref[...] = jnp.zeros_like(o_ref)

    h = h_ref[...]
    g = _dot(h, wg_ref[...])
    u = _dot(h, wu_ref[...])
    a = (_silu(g) * u).astype(bf16)
    o_ref[...] += _dot(a, wd_ref[...])

    @pl.when(f == pl.num_programs(1) - 1)
    def _():
        gate = mod_ref[0, 5:6, :]
        fgain = fgain_ref[...]

        def body(i, carry):
            sl = pl.ds(pl.multiple_of(i * rows, rows), rows)
            y = x_ref[sl, :] + gate * o_ref[sl, :]
            if final_norm:
                ms = jnp.mean(y * y, axis=-1, keepdims=True)
                y = (y * lax.rsqrt(ms + EPS)) * fgain
            o_ref[sl, :] = y
            return carry

        lax.fori_loop(0, o_ref.shape[0] // rows, body, 0)


def _ffn(x2, mods, gains, fgain, wg, wu, wd, l, seq, final_norm, tm=1024, tf=512):
    m, d = x2.shape
    dff = wg.shape[2]
    per_b = seq // tm
    return pl.pallas_call(
        functools.partial(_ffn_kernel, final_norm=final_norm),
        grid=(m // tm, dff // tf),
        in_specs=[
            pl.BlockSpec((tm, d), lambda i, j: (i, 0)),
            pl.BlockSpec((None, 1, 6, d), lambda i, j: (l, i // per_b, 0, 0)),
            pl.BlockSpec((None, 1, d), lambda i, j: (l, 0, 0)),
            pl.BlockSpec((1, d), lambda i, j: (0, 0)),
            pl.BlockSpec((None, d, tf), lambda i, j: (l, 0, j)),
            pl.BlockSpec((None, d, tf), lambda i, j: (l, 0, j)),
            pl.BlockSpec((None, tf, d), lambda i, j: (l, j, 0)),
        ],
        out_specs=pl.BlockSpec((tm, d), lambda i, j: (i, 0)),
        out_shape=jax.ShapeDtypeStruct((m, d), f32),
        scratch_shapes=[pltpu.VMEM((tm, d), bf16)],
        compiler_params=_cp("parallel", "arbitrary"),
        name="ffn",
    )(x2, mods, gains, fgain.reshape(1, d), wg, wu, wd)


def _rope_kernel(pos_ref, invf_ref, sgn_ref, cos_ref, sin_ref):
    ang = pos_ref[...].astype(f32) * invf_ref[...]
    cos_ref[...] = jnp.cos(ang)
    sin_ref[...] = jnp.sin(ang) * sgn_ref[...]


def _rope_tables(positions):
    m = positions.size
    half = ROPE_DIMS // 2
    inv_freq = ROPE_THETA ** (-jnp.arange(half, dtype=f32) / half)
    dim = np.arange(LANES) % SWA_HD
    invf = jnp.where(jnp.asarray(dim < ROPE_DIMS), inv_freq[jnp.asarray(dim % half)], 0.0).reshape(1, LANES)
    sgn = jnp.asarray(np.where(dim < half, -1.0, 1.0), dtype=f32).reshape(1, LANES)
    pos_b = jnp.broadcast_to(positions.reshape(m, 1), (m, LANES))
    tr = 2048
    return pl.pallas_call(
        _rope_kernel,
        grid=(m // tr,),
        in_specs=[
            pl.BlockSpec((tr, LANES), lambda i: (i, 0)),
            pl.BlockSpec((1, LANES), lambda i: (0, 0)),
            pl.BlockSpec((1, LANES), lambda i: (0, 0)),
        ],
        out_specs=[pl.BlockSpec((tr, LANES), lambda i: (i, 0))] * 2,
        out_shape=[jax.ShapeDtypeStruct((m, LANES), f32)] * 2,
        compiler_params=_cp("parallel"),
        name="rope_tables",
    )(pos_b, invf, sgn)


def _swa_kernel(sink_ref, q_ref, kv_ref, cos_ref, sin_ref, o_ref, kvprev_ref):
    blk = pl.program_id(1)
    w = WINDOW

    @pl.when(blk == 0)
    def _():
        kvprev_ref[...] = jnp.zeros_like(kvprev_ref)

    lane = lax.broadcasted_iota(jnp.int32, (1, LANES), 1)
    low_half = (lane % SWA_HD) < (ROPE_DIMS // 2)
    first_head = lane < SWA_HD
    qi = lax.broadcasted_iota(jnp.int32, (w, 2 * w), 0)
    kj = lax.broadcasted_iota(jnp.int32, (w, 2 * w), 1)
    dist = qi + w - kj
    band = (dist >= 0) & (dist < w)
    group = SWA_HEADS // SWA_KV_HEADS
    batch = 4

    kprev = kvprev_ref[:, :LANES]
    vprev = kvprev_ref[:, LANES:]
    for sb in range(q_ref.shape[0] // w):
        rs = slice(sb * w, (sb + 1) * w)
        cos = cos_ref[rs, :]
        sin = sin_ref[rs, :]

        def rot(x):
            swapped = jnp.where(low_half, pltpu.roll(x, LANES - ROPE_DIMS // 2, 1), pltpu.roll(x, ROPE_DIMS // 2, 1))
            return x * cos + swapped * sin

        kcur = rot(kv_ref[rs, :LANES])
        vcur = kv_ref[rs, LANES:]
        kk = jnp.concatenate([kprev, kcur], axis=0)
        vv = jnp.concatenate([vprev, vcur], axis=0)
        kk_sw = pltpu.roll(kk, SWA_HD, 1)
        vv_sw = pltpu.roll(vv, SWA_HD, 1)
        kdup = [jnp.where(first_head, kk, kk_sw).astype(bf16), jnp.where(first_head, kk_sw, kk).astype(bf16)]
        vlo = [jnp.where(first_head, vv, 0.0).astype(bf16), jnp.where(first_head, vv_sw, 0.0).astype(bf16)]
        vhi = [jnp.where(first_head, 0.0, vv_sw).astype(bf16), jnp.where(first_head, 0.0, vv).astype(bf16)]
        valid = band if sb > 0 else band & ((kj >= w) | (blk > 0))

        for t0 in range(0, SWA_HEADS // 2, batch):
            tiles = range(t0, t0 + batch)
            heads = [2 * t + half for t in tiles for half in range(2)]
            qts = [rot(q_ref[rs, t * LANES:(t + 1) * LANES]) * (SWA_HD ** -0.5) for t in tiles]
            qhs = [jnp.where(first_head if h % 2 == 0 else ~first_head, qts[h // 2 - t0], 0.0).astype(bf16)
                   for h in heads]
            ss = [jnp.where(valid, _dot_nt(qh, kdup[h // group]), -jnp.inf) for qh, h in zip(qhs, heads)]
            mxs = [jnp.maximum(jnp.max(s, axis=-1, keepdims=True), sink_ref[h]) for s, h in zip(ss, heads)]
            ps = [jnp.exp(s - mx) for s, mx in zip(ss, mxs)]
            dens = [jnp.sum(p, axis=-1, keepdims=True) + jnp.exp(sink_ref[h] - mx)
                    for p, mx, h in zip(ps, mxs, heads)]
            os_ = [_dot(p.astype(bf16), (vlo if h % 2 == 0 else vhi)[h // group]) / den
                   for p, den, h in zip(ps, dens, heads)]
            for t in tiles:
                k = 2 * (t - t0)
                o_ref[rs, t * LANES:(t + 1) * LANES] = (os_[k] + os_[k + 1]).astype(o_ref.dtype)
        kprev, vprev = kcur, vcur

    kvprev_ref[:, :LANES] = kprev
    kvprev_ref[:, LANES:] = vprev


def _swa(proj, cos_t, sin_t, sinks, bsz, seq, windows_per_step=2):
    m = proj.shape[0]
    tb = windows_per_step * WINDOW
    nb = seq // tb
    qw = SWA_HEADS * SWA_HD
    kvw = 2 * SWA_KV_HEADS * SWA_HD
    row = lambda b, i: b * nb + i
    return pl.pallas_call(
        _swa_kernel,
        grid=(bsz, nb),
        in_specs=[
            pl.BlockSpec(memory_space=pltpu.SMEM),
            pl.BlockSpec((tb, qw), lambda b, i: (row(b, i), COL_BQ // qw)),
            pl.BlockSpec((tb, kvw), lambda b, i: (row(b, i), COL_BKV // kvw)),
            pl.BlockSpec((tb, LANES), lambda b, i: (row(b, i), 0)),
            pl.BlockSpec((tb, LANES), lambda b, i: (row(b, i), 0)),
        ],
        out_specs=pl.BlockSpec((tb, qw), lambda b, i: (row(b, i), 0)),
        out_shape=jax.ShapeDtypeStruct((m, qw), bf16),
        scratch_shapes=[pltpu.VMEM((WINDOW, kvw), f32)],
        compiler_params=_cp("parallel", "arbitrary"),
        name="swa",
    )(sinks, proj, proj, cos_t, sin_t)


_GLA_LEVELS = (32, 16, 8, 4, 2, 1)


def _gla_constants():
    c = CHUNK
    t = np.arange(c)
    tri = (t[:, None] >= t[None, :]).astype(np.float32)
    sel = np.zeros((len(_GLA_LEVELS) * c, c), np.float32)
    for l, mhalf in enumerate(_GLA_LEVELS):
        mid = (t // (2 * mhalf)) * 2 * mhalf + mhalf - 1
        sel[l * c + t, mid] = 1.0
    cum_sel = np.concatenate([tri, sel @ tri], axis=0)
    n = GLA_HEADS * c
    idx = np.arange(n)
    head, tt = idx // c, idx % c
    same = head[:, None] == head[None, :]
    level = np.full((n, n), -1, np.int32)
    for l, mhalf in enumerate(_GLA_LEVELS):
        blk = (tt[:, None] // (2 * mhalf)) == (tt[None, :] // (2 * mhalf))
        msk = same & blk & ((tt[:, None] % (2 * mhalf)) >= mhalf) & ((tt[None, :] % (2 * mhalf)) < mhalf)
        level[msk] = l
    level[same & (tt[:, None] == tt[None, :])] = len(_GLA_LEVELS)
    return jnp.asarray(cum_sel, dtype=bf16), jnp.asarray(level)


def _stack_heads(x, width, heads):
    return jnp.concatenate([x[:, h * width:(h + 1) * width] for h in range(heads)], axis=0)


def _split_hi_lo(x):
    hi = x.astype(bf16)
    return hi, (x - hi.astype(f32)).astype(bf16)


def _gla_kernel(q_ref, k_ref, v_ref, g_ref, sm_ref, wgk_ref, bgk_ref, gain_ref, cumsel_ref, lv_ref, o_ref, st_ref):
    c = CHUNK

    @pl.when(pl.program_id(1) == 0)
    def _():
        st_ref[...] = jnp.zeros_like(st_ref)

    lane = lax.broadcasted_iota(jnp.int32, (1, GLA_HEADS * GLA_DK), 1)
    head_mask = [(lane // GLA_DK) == h for h in range(GLA_HEADS)]

    def per_head_rows(x):
        x16 = x.astype(bf16)
        return jnp.concatenate([jnp.where(head_mask[h], x16, jnp.zeros_like(x16)) for h in range(GLA_HEADS)], axis=0)

    log_a_all = jax.nn.log_sigmoid(_dot(sm_ref[...].astype(bf16), wgk_ref[...]) + bgk_ref[...]) * (1.0 / GLA_NORMALIZER)
    cum_sel = cumsel_ref[...]
    lv = lv_ref[...]
    subs = range(q_ref.shape[0] // c)
    rows = [slice(i * c, (i + 1) * c) for i in subs]
    pair = 2 * c
    qs = [q_ref[r, :] * (GLA_DK ** -0.5) for r in rows]
    ks = [k_ref[r, :] for r in rows]
    bbs = []
    for r in rows:
        la_hi, la_lo = _split_hi_lo(log_a_all[r, :])
        bbs.append(_dot(cum_sel, la_hi) + _dot(cum_sel, la_lo))
    bs = [bb[:c, :] for bb in bbs]
    accs = [[jnp.zeros((c, pair), f32) for _ in range(GLA_HEADS)] for _ in subs]
    for l in range(len(_GLA_LEVELS) + 1):
        for i in subs:
            q, k, b = qs[i], ks[i], bs[i]
            if l < len(_GLA_LEVELS):
                bm = bbs[i][(l + 1) * c:(l + 2) * c, :]
                qe = q * jnp.exp(jnp.minimum(b - bm, 0.0))
                ke = k * jnp.exp(jnp.minimum(bm - b, 0.0))
            else:
                qe, ke = q, k
            ke16 = ke.astype(bf16)
            res = _dot_nt(per_head_rows(qe), jnp.concatenate([ke16] * GLA_HEADS, axis=0))
            for h in range(GLA_HEADS):
                hr = slice(h * c, (h + 1) * c)
                hc = slice((h // 2) * pair, (h // 2 + 1) * pair)
                accs[i][h] = jnp.where(lv[hr, hc] == l, res[hr, hc], accs[i][h])
    vsts = [_stack_heads(v_ref[r, :], GLA_DV, GLA_HEADS).astype(bf16) for r in rows]
    o_intra = []
    for i in subs:
        parts = []
        for hp in range(GLA_HEADS // 2):
            a_pair = jnp.concatenate([accs[i][2 * hp], accs[i][2 * hp + 1]], axis=0).astype(bf16)
            parts.append(_dot(a_pair, vsts[i][hp * pair:(hp + 1) * pair, :]))
        o_intra.append(jnp.concatenate(parts, axis=0))
    qgs = [per_head_rows(q * jnp.exp(b)) for q, b in zip(qs, bs)]
    kds = [per_head_rows(k * jnp.exp(b[c - 1:c, :] - b)) for k, b in zip(ks, bs)]

    for i in subs:
        r, vst = rows[i], vsts[i]
        st = st_ref[...]
        o = o_intra[i] + _dot_nt(qgs[i], st.astype(bf16))
        st_ref[...] = st * jnp.exp(bs[i][c - 1:c, :]) + _dot_tn(vst, kds[i])
        ms = jnp.mean(o * o, axis=-1, keepdims=True)
        y = (o * lax.rsqrt(ms + EPS)) * gain_ref[...]
        y = y * _silu(_stack_heads(g_ref[r, :], GLA_DV, GLA_HEADS))
        for h in range(GLA_HEADS):
            o_ref[r, h * GLA_DV:(h + 1) * GLA_DV] = y[h * c:(h + 1) * c, :].astype(o_ref.dtype)


def _gla(proj, w_gk, b_gk, gain, bsz, seq, tb=512):
    m = proj.shape[0]
    nb = seq // tb
    qk_w = GLA_HEADS * GLA_DK
    v_w = GLA_HEADS * GLA_DV
    cum_sel, level = _gla_constants()
    wgk_pad = jnp.zeros((LANES, qk_w), f32).at[SM_ALR:SM_ALR + GLA_RANK].set(w_gk).astype(bf16)
    row = lambda b, i: b * nb + i
    const = lambda shape: pl.BlockSpec(shape, lambda b, i: (0,) * len(shape))
    return pl.pallas_call(
        _gla_kernel,
        grid=(bsz, nb),
        in_specs=[
            pl.BlockSpec((tb, qk_w), lambda b, i: (row(b, i), COL_AQ // qk_w)),
            pl.BlockSpec((tb, qk_w), lambda b, i: (row(b, i), COL_AK // qk_w)),
            pl.BlockSpec((tb, v_w), lambda b, i: (row(b, i), COL_AV // v_w)),
            pl.BlockSpec((tb, v_w), lambda b, i: (row(b, i), COL_AG // v_w)),
            pl.BlockSpec((tb, LANES), lambda b, i: (row(b, i), COL_SMALL // LANES)),
            const((LANES, qk_w)), const((1, qk_w)), const((1, GLA_DV)),
            const(cum_sel.shape), const(level.shape),
        ],
        out_specs=pl.BlockSpec((tb, v_w), lambda b, i: (row(b, i), 0)),
        out_shape=jax.ShapeDtypeStruct((m, v_w), bf16),
        scratch_shapes=[pltpu.VMEM((GLA_DV, qk_w), f32)],
        compiler_params=_cp("parallel", "arbitrary"),
        name="gla",
    )(proj, proj, proj, proj, proj, wgk_pad, b_gk.reshape(1, qk_w), gain.reshape(1, GLA_DV), cum_sel, level)


_GDN_GROUP = 2


def _gdn_constants():
    c = CHUNK
    n = _GDN_GROUP * c
    idx = np.arange(n)
    head, tt = idx // c, idx % c
    same = head[:, None] == head[None, :]
    t = np.arange(c)
    tri = (t[:, None] >= t[None, :]).astype(np.float32)
    kind = np.zeros((n, n), np.int32)
    kind[same & (tt[:, None] == tt[None, :])] = 1
    kind[same & (tt[:, None] > tt[None, :])] = 2
    return jnp.asarray(tri, dtype=bf16), jnp.asarray(kind)


def _gdn_kernel(alog_ref, dtb_ref, cq_ref, ck_ref, cv_ref, cz_ref, sm_ref, cw_ref, gain_ref, tri_ref, kind_ref,
                o_ref, carry_ref, qkv_ref, s_ref):
    c = CHUNK
    tb = cq_ref.shape[0]
    hw = GDN_HEADS * GDN_DK

    @pl.when(pl.program_id(1) == 0)
    def _():
        carry_ref[...] = jnp.zeros_like(carry_ref)
        s_ref[...] = jnp.zeros_like(s_ref)

    row8 = lax.broadcasted_iota(jnp.int32, (8, LANES), 0)
    for part, src in enumerate((cq_ref, ck_ref, cv_ref)):
        for h in range(GDN_HEADS):
            col = part * hw + h * LANES
            x = src[:, h * LANES:(h + 1) * LANES]
            prev = carry_ref[:, col:col + LANES]
            acc = x * cw_ref[CONV_WIDTH - 1:CONV_WIDTH, col:col + LANES]
            for shift in range(1, CONV_WIDTH):
                xs = pltpu.roll(x, shift, 0)
                top = jnp.where(row8 < shift, pltpu.roll(prev, shift, 0), xs[0:8, :])
                xs = jnp.concatenate([top, xs[8:, :]], axis=0)
                acc = acc + xs * cw_ref[CONV_WIDTH - 1 - shift:CONV_WIDTH - shift, col:col + LANES]
            carry_ref[:, col:col + LANES] = x[tb - 8:tb, :]
            y = _silu(acc)
            if part < 2:
                y = y * lax.rsqrt(jnp.sum(y * y, axis=-1, keepdims=True) + EPS)
            if part == 0:
                y = y * (GDN_DK ** -0.5)
            qkv_ref[:, col:col + LANES] = y

    sm_all = sm_ref[...]
    g_all = -jnp.exp(alog_ref[...]) * jax.nn.softplus(sm_all + dtb_ref[...])
    beta_all = jax.nn.sigmoid(sm_all)
    tri = tri_ref[...]
    kind = kind_ref[...]

    def lane_to_rows(x, lane0):
        return jnp.concatenate(
            [jnp.broadcast_to(x[:, lane0 + h:lane0 + h + 1], (c, LANES)) for h in range(_GDN_GROUP)], axis=0)

    subs = range(tb // c)
    groups = range(GDN_HEADS // _GDN_GROUP)
    gw = _GDN_GROUP * GDN_DK
    units = [(i, gp) for i in subs for gp in groups]
    rows = [slice(i * c, (i + 1) * c) for i in subs]

    def cumsum3(g_f):
        g_hi = g_f.astype(bf16)
        g_mid, g_lo = _split_hi_lo(g_f - g_hi.astype(f32))
        return _dot(tri, g_hi) + (_dot(tri, g_mid) + _dot(tri, g_lo))

    def pair(i, gp, part):
        lo = part * hw + gp * gw
        return _stack_heads(qkv_ref[rows[i], lo:lo + gw], GDN_DK, _GDN_GROUP)

    gc_lanes = [cumsum3(g_all[r, :]) for r in rows]
    gcs = [lane_to_rows(gc_lanes[i], SM_CA + gp * _GDN_GROUP) for i, gp in units]
    betas = [lane_to_rows(beta_all[rows[i], :], SM_CB + gp * _GDN_GROUP) for i, gp in units]
    decs = [jnp.exp(jnp.minimum(gc - gc.T[0:1, :], 0.0)) for gc in gcs]
    qsts = [pair(i, gp, 0) for i, gp in units]
    ksts = [pair(i, gp, 1) for i, gp in units]
    vsts = [pair(i, gp, 2) for i, gp in units]
    egcs = [jnp.exp(gc) for gc in gcs]
    kbs = [kst * beta for kst, beta in zip(ksts, betas)]
    k16s = [kst.astype(bf16) for kst in ksts]
    m_negs = [jnp.where(kind == 2, -(_dot_nt(kb.astype(bf16), k16) * dec), 0.0)
              for kb, k16, dec in zip(kbs, k16s, decs)]
    attns = [jnp.where(kind >= 1, _dot_nt(qst.astype(bf16), k16) * dec, 0.0).astype(bf16)
             for qst, k16, dec in zip(qsts, k16s, decs)]
    rhss = [jnp.concatenate([vst * beta, kb * egc], axis=1)
            for vst, beta, kb, egc in zip(vsts, betas, kbs, egcs)]
    m_his = [m.astype(bf16) for m in m_negs]
    m_los = [(m - mh.astype(f32)).astype(bf16) for m, mh in zip(m_negs, m_his)]
    ps = m_his
    tinvs = m_negs
    for _ in range(4):
        ps = [_dot(p, p).astype(bf16) for p in ps]
        tinvs = [tinv + _dot(tinv.astype(bf16), p) + p.astype(f32) for tinv, p in zip(tinvs, ps)]
    t16s = [tinv.astype(bf16) for tinv in tinvs]
    x0s = [(rhs + _dot(t16, rhs.astype(bf16))).astype(bf16) for rhs, t16 in zip(rhss, t16s)]
    resids = [(rhs - x0.astype(f32)) + (_dot(mh, x0) + _dot(ml, x0))
              for rhs, x0, mh, ml in zip(rhss, x0s, m_his, m_los)]
    xss = [x0.astype(f32) + (resid + _dot(t16, resid.astype(bf16))) for x0, resid, t16 in zip(x0s, resids, t16s)]
    qgs = [qst * egc for qst, egc in zip(qsts, egcs)]

    for u, (i, gp) in enumerate(units):
        r, gc, kst, xs, qg, attn = rows[i], gcs[u], ksts[u], xss[u], qgs[u], attns[u]
        heads = [gp * _GDN_GROUP + hl for hl in range(_GDN_GROUP)]
        vnew_parts, oq_parts = [], []
        for hl, h in enumerate(heads):
            hs = slice(hl * c, (hl + 1) * c)
            s16 = s_ref[h].astype(bf16)
            t = _dot(jnp.concatenate([xs[hs, GDN_DV:], qg[hs, :]], axis=0).astype(bf16), s16)
            vnew_parts.append(xs[hs, :GDN_DV] - t[:c])
            oq_parts.append(t[c:])
        vnew = jnp.concatenate(vnew_parts, axis=0)
        v16 = vnew.astype(bf16)
        o = jnp.concatenate(oq_parts, axis=0) + _dot(attn, v16)
        for hl, h in enumerate(heads):
            hs = slice(hl * c, (hl + 1) * c)
            g_last = gc[(hl + 1) * c - 1:(hl + 1) * c, :]
            kd = (kst[hs, :] * jnp.exp(g_last - gc[hs, :])).astype(bf16)
            s_ref[h] = s_ref[h] * jnp.exp(g_last) + _dot_tn(kd, v16[hs, :])
        ms = jnp.mean(o * o, axis=-1, keepdims=True)
        y = (o * lax.rsqrt(ms + EPS)) * gain_ref[...]
        y = y * _silu(_stack_heads(cz_ref[r, gp * gw:(gp + 1) * gw], GDN_DV, _GDN_GROUP))
        for hl, h in enumerate(heads):
            o_ref[r, h * GDN_DV:(h + 1) * GDN_DV] = y[hl * c:(hl + 1) * c, :].astype(o_ref.dtype)


def _gdn(proj, conv_w, a_log, dt_bias, gain, bsz, seq, tb=512):
    m = proj.shape[0]
    nb = seq // tb
    hw = GDN_HEADS * GDN_DK
    tri, kind = _gdn_constants()
    row = lambda b, i: b * nb + i
    const = lambda shape: pl.BlockSpec(shape, lambda b, i: (0,) * len(shape))
    a_log = jnp.zeros((1, LANES), f32).at[0, SM_CA:SM_CA + GDN_HEADS].set(a_log)
    dt_bias = jnp.zeros((1, LANES), f32).at[0, SM_CA:SM_CA + GDN_HEADS].set(dt_bias)
    return pl.pallas_call(
        _gdn_kernel,
        grid=(bsz, nb),
        in_specs=[
            const((1, LANES)), const((1, LANES)),
            pl.BlockSpec((tb, hw), lambda b, i: (row(b, i), COL_CQ // hw)),
            pl.BlockSpec((tb, hw), lambda b, i: (row(b, i), COL_CK // hw)),
            pl.BlockSpec((tb, hw), lambda b, i: (row(b, i), COL_CV // hw)),
            pl.BlockSpec((tb, hw), lambda b, i: (row(b, i), COL_CZ // hw)),
            pl.BlockSpec((tb, LANES), lambda b, i: (row(b, i), COL_SMALL // LANES)),
            const((CONV_WIDTH, 3 * hw)), const((1, GDN_DV)), const(tri.shape), const(kind.shape),
        ],
        out_specs=pl.BlockSpec((tb, hw), lambda b, i: (row(b, i), 0)),
        out_shape=jax.ShapeDtypeStruct((m, hw), bf16),
        scratch_shapes=[
            pltpu.VMEM((8, 3 * hw), f32),
            pltpu.VMEM((tb, 3 * hw), f32),
            pltpu.VMEM((GDN_HEADS, GDN_DK, GDN_DV), f32),
        ],
        compiler_params=_cp("parallel", "arbitrary"),
        name="gdn",
    )(a_log, dt_bias, proj, proj, proj, proj, proj, conv_w, gain.reshape(1, GDN_DV), tri, kind)


def _permute_w_in(w_in):
    depth, d, _ = w_in.shape
    pieces = [
        w_in[:, :, 1552:2576],
        w_in[:, :, 512:1536],
        w_in[:, :, 2832:4880],
        w_in[:, :, 0:512],
        w_in[:, :, 2576:2832],
        w_in[:, :, 1536:1552],
        w_in[:, :, 4880:4888],
    ]
    used = sum(p.shape[-1] for p in pieces)
    pieces = [p.astype(bf16) for p in pieces]
    pieces.append(jnp.zeros((depth, d, PROJ_WIDTH - used), bf16))
    return jnp.concatenate(pieces, axis=-1)


def kernel(x, c, positions, w_mod, b_mod, norm1_gain, norm2_gain, w_in, gla_w_gk, gla_b_gk, gla_norm_gain,
           swa_sinks, gdn_conv_w, gdn_a_log, gdn_dt_bias, gdn_norm_gain, w_out, ffn_w_gate, ffn_w_up, ffn_w_down,
           final_norm_gain):
    bsz, seq, d = x.shape
    depth = w_in.shape[0]
    mods = _modulation(c, w_mod, b_mod)
    cos_t, sin_t = _rope_tables(positions)
    w_in_p = _permute_w_in(w_in)
    w_out16 = w_out.astype(bf16)
    wg16, wu16, wd16 = ffn_w_gate.astype(bf16), ffn_w_up.astype(bf16), ffn_w_down.astype(bf16)
    x2 = x.reshape(bsz * seq, d)
    g1 = norm1_gain.reshape(depth, 1, d)
    g2 = norm2_gain.reshape(depth, 1, d)
    for l in range(depth):
        proj = _in_proj(x2, mods, g1, w_in_p, l, seq)
        o_a = _gla(proj, gla_w_gk[l], gla_b_gk[l], gla_norm_gain[l], bsz, seq)
        o_b = _swa(proj, cos_t, sin_t, swa_sinks[l], bsz, seq)
        o_c = _gdn(proj, gdn_conv_w[l], gdn_a_log[l], gdn_dt_bias[l], gdn_norm_gain[l], bsz, seq)
        x2 = _out_proj(x2, o_a, o_b, o_c, w_out16, mods, l, seq)
        x2 = _ffn(x2, mods, g2, final_norm_gain, wg16, wu16, wd16, l, seq, final_norm=(l == depth - 1))
    return x2.reshape(bsz, seq, d)
```

```python
import functools

import numpy as np
import jax
import jax.numpy as jnp
from jax import lax
from jax.experimental import pallas as pl
from jax.experimental.pallas import tpu as pltpu

f32 = jnp.float32
bf16 = jnp.bfloat16

EPS = 1e-6
GLA_HEADS, GLA_DK, GLA_DV, GLA_RANK = 4, 64, 128, 16
GLA_NORMALIZER = 16.0
SWA_HEADS, SWA_KV_HEADS, SWA_HD, WINDOW = 16, 2, 64, 128
ROPE_THETA, ROPE_DIMS = 500000.0, 16
GDN_HEADS, GDN_DK, GDN_DV, CONV_WIDTH = 4, 128, 128, 4
CHUNK = 64
LANES = 128

COL_BQ, COL_AV, COL_AG = 0, 1024, 1536
COL_CQ, COL_CK, COL_CV, COL_CZ = 2048, 2560, 3072, 3584
COL_AQ, COL_AK, COL_BKV, COL_SMALL = 4096, 4352, 4608, 4864
PROJ_WIDTH = 5120
SM_ALR, SM_CB, SM_CA = 0, 16, 20

VMEM_LIMIT = 56 * 1024 * 1024


def _cp(*sem):
    return pltpu.CompilerParams(dimension_semantics=sem, vmem_limit_bytes=VMEM_LIMIT)


def _dot(a, b):
    return jnp.dot(a, b, preferred_element_type=f32)


def _dot_nt(a, b):
    return lax.dot_general(a, b, (((1,), (1,)), ((), ())), preferred_element_type=f32)


def _dot_tn(a, b):
    return lax.dot_general(a, b, (((0,), (0,)), ((), ())), preferred_element_type=f32)


def _silu(x):
    return x * jax.nn.sigmoid(x)


def _norm_mod_rows(x_ref, gain, scale, shift, h_ref, rows):
    n = x_ref.shape[0] // rows
    gs = gain * (1.0 + scale)

    def body(i, carry):
        sl = pl.ds(pl.multiple_of(i * rows, rows), rows)
        x = x_ref[sl, :]
        ms = jnp.mean(x * x, axis=-1, keepdims=True)
        h_ref[sl, :] = ((x * lax.rsqrt(ms + EPS)) * gs + shift).astype(h_ref.dtype)
        return carry

    lax.fori_loop(0, n, body, 0)


def _mod_kernel(c_ref, w_ref, b_ref, o_ref):
    c = c_ref[...]
    o_ref[0] = _dot(_silu(c).astype(bf16), w_ref[0].astype(bf16)) + b_ref[0]


def _modulation(c, w_mod, b_mod):
    depth, d, n = w_mod.shape
    bsz = c.shape[0]
    rows = 8
    c_pad = jnp.pad(c, ((0, rows - bsz), (0, 0)))
    tn = 1024
    out = pl.pallas_call(
        _mod_kernel,
        grid=(depth, n // tn),
        in_specs=[
            pl.BlockSpec((rows, d), lambda l, j: (0, 0)),
            pl.BlockSpec((1, d, tn), lambda l, j: (l, 0, j)),
            pl.BlockSpec((1, 1, tn), lambda l, j: (l, 0, j)),
        ],
        out_specs=pl.BlockSpec((1, rows, tn), lambda l, j: (l, 0, j)),
        out_shape=jax.ShapeDtypeStruct((depth, rows, n), f32),
        compiler_params=_cp("parallel", "parallel"),
        name="modulation",
    )(c_pad, w_mod, b_mod.reshape(depth, 1, n))
    return out[:, :bsz, :].reshape(depth, bsz, 6, d)


def _in_proj_kernel(x_ref, mod_ref, gain_ref, w_ref, o_ref, h_ref):
    @pl.when(pl.program_id(1) == 0)
    def _():
        _norm_mod_rows(x_ref, gain_ref[...], mod_ref[0, 1:2, :], mod_ref[0, 0:1, :], h_ref, 128)

    o_ref[...] = _dot(h_ref[...], w_ref[...])


def _in_proj(x2, mods, gains, w, l, seq, tm=1024, tn=1280):
    m, d = x2.shape
    n = w.shape[2]
    per_b = seq // tm
    return pl.pallas_call(
        _in_proj_kernel,
        grid=(m // tm, n // tn),
        in_specs=[
            pl.BlockSpec((tm, d), lambda i, j: (i, 0)),
            pl.BlockSpec((None, 1, 6, d), lambda i, j: (l, i // per_b, 0, 0)),
            pl.BlockSpec((None, 1, d), lambda i, j: (l, 0, 0)),
            pl.BlockSpec((None, d, tn), lambda i, j: (l, 0, j)),
        ],
        out_specs=pl.BlockSpec((tm, tn), lambda i, j: (i, j)),
        out_shape=jax.ShapeDtypeStruct((m, n), f32),
        scratch_shapes=[pltpu.VMEM((tm, d), bf16)],
        compiler_params=_cp("parallel", "arbitrary"),
        name="in_proj",
    )(x2, mods, gains, w)


def _out_proj_kernel(x_ref, oa_ref, ob_ref, oc_ref, w_ref, mod_ref, o_ref):
    ka, kb = oa_ref.shape[1], ob_ref.shape[1]
    acc = _dot(oa_ref[...], w_ref[0:ka, :])
    acc = acc + _dot(ob_ref[...], w_ref[ka:ka + kb, :])
    acc = acc + _dot(oc_ref[...], w_ref[ka + kb:, :])
    o_ref[...] = x_ref[...] + mod_ref[0, 2:3, :] * acc


def _out_proj(x2, oa, ob, oc, w_out, mods, l, seq, tm=512):
    m, d = x2.shape
    per_b = seq // tm
    ka, kb, kc = oa.shape[1], ob.shape[1], oc.shape[1]
    return pl.pallas_call(
        _out_proj_kernel,
        grid=(m // tm,),
        in_specs=[
            pl.BlockSpec((tm, d), lambda i: (i, 0)),
            pl.BlockSpec((tm, ka), lambda i: (i, 0)),
            pl.BlockSpec((tm, kb), lambda i: (i, 0)),
            pl.BlockSpec((tm, kc), lambda i: (i, 0)),
            pl.BlockSpec((None, ka + kb + kc, d), lambda i: (l, 0, 0)),
            pl.BlockSpec((None, 1, 6, d), lambda i: (l, i // per_b, 0, 0)),
        ],
        out_specs=pl.BlockSpec((tm, d), lambda i: (i, 0)),
        out_shape=jax.ShapeDtypeStruct((m, d), f32),
        compiler_params=_cp("parallel"),
        name="out_proj",
    )(x2, oa, ob, oc, w_out, mods)


def _ffn_kernel(x_ref, mod_ref, gain_ref, fgain_ref, wg_ref, wu_ref, wd_ref, o_ref, h_ref, *, final_norm):
    f = pl.program_id(1)
    rows = 128

    @pl.when(f == 0)
    def _():
        _norm_mod_rows(x_ref, gain_ref[...], mod_ref[0, 4:5, :], mod_ref[0, 3:4, :], h_ref, rows)
        o_ref[...] = jnp.zeros_like(o_ref)

    h = h_ref[...]
    g = _dot(h, wg_ref[...])
    u = _dot(h, wu_ref[...])
    a = (_silu(g) * u).astype(bf16)
    o_ref[...] += _dot(a, wd_ref[...])

    @pl.when(f == pl.num_programs(1) - 1)
    def _():
        gate = mod_ref[0, 5:6, :]
        fgain = fgain_ref[...]

        def body(i, carry):
            sl = pl.ds(pl.multiple_of(i * rows, rows), rows)
            y = x_ref[sl, :] + gate * o_ref[sl, :]
            if final_norm:
                ms = jnp.mean(y * y, axis=-1, keepdims=True)
                y = (y * lax.rsqrt(ms + EPS)) * fgain
            o_ref[sl, :] = y
            return carry

        lax.fori_loop(0, o_ref.shape[0] // rows, body, 0)


def _ffn(x2, mods, gains, fgain, wg, wu, wd, l, seq, final_norm, tm=1024, tf=512):
    m, d = x2.shape
    dff = wg.shape[2]
    per_b = seq // tm
    return pl.pallas_call(
        functools.partial(_ffn_kernel, final_norm=final_norm),
        grid=(m // tm, dff // tf),
        in_specs=[
            pl.BlockSpec((tm, d), lambda i, j: (i, 0)),
            pl.BlockSpec((None, 1, 6, d), lambda i, j: (l, i // per_b, 0, 0)),
            pl.BlockSpec((None, 1, d), lambda i, j: (l, 0, 0)),
            pl.BlockSpec((1, d), lambda i, j: (0, 0)),
            pl.BlockSpec((None, d, tf), lambda i, j: (l, 0, j)),
            pl.BlockSpec((None, d, tf), lambda i, j: (l, 0, j)),
            pl.BlockSpec((None, tf, d), lambda i, j: (l, j, 0)),
        ],
        out_specs=pl.BlockSpec((tm, d), lambda i, j: (i, 0)),
        out_shape=jax.ShapeDtypeStruct((m, d), f32),
        scratch_shapes=[pltpu.VMEM((tm, d), bf16)],
        compiler_params=_cp("parallel", "arbitrary"),
        name="ffn",
    )(x2, mods, gains, fgain.reshape(1, d), wg, wu, wd)


def _rope_kernel(pos_ref, invf_ref, sgn_ref, cos_ref, sin_ref):
    ang = pos_ref[...].astype(f32) * invf_ref[...]
    cos_ref[...] = jnp.cos(ang)
    sin_ref[...] = jnp.sin(ang) * sgn_ref[...]


def _rope_tables(positions):
    m = positions.size
    half = ROPE_DIMS // 2
    inv_freq = ROPE_THETA ** (-jnp.arange(half, dtype=f32) / half)
    dim = np.arange(LANES) % SWA_HD
    invf = jnp.where(jnp.asarray(dim < ROPE_DIMS), inv_freq[jnp.asarray(dim % half)], 0.0).reshape(1, LANES)
    sgn = jnp.asarray(np.where(dim < half, -1.0, 1.0), dtype=f32).reshape(1, LANES)
    pos_b = jnp.broadcast_to(positions.reshape(m, 1), (m, LANES))
    tr = 2048
    return pl.pallas_call(
        _rope_kernel,
        grid=(m // tr,),
        in_specs=[
            pl.BlockSpec((tr, LANES), lambda i: (i, 0)),
            pl.BlockSpec((1, LANES), lambda i: (0, 0)),
            pl.BlockSpec((1, LANES), lambda i: (0, 0)),
        ],
        out_specs=[pl.BlockSpec((tr, LANES), lambda i: (i, 0))] * 2,
        out_shape=[jax.ShapeDtypeStruct((m, LANES), f32)] * 2,
        compiler_params=_cp("parallel"),
        name="rope_tables",
    )(pos_b, invf, sgn)


def _swa_kernel(sink_ref, q_ref, kv_ref, cos_ref, sin_ref, o_ref, kvprev_ref):
    blk = pl.program_id(1)
    w = WINDOW

    @pl.when(blk == 0)
    def _():
        kvprev_ref[...] = jnp.zeros_like(kvprev_ref)

    lane = lax.broadcasted_iota(jnp.int32, (1, LANES), 1)
    low_half = (lane % SWA_HD) < (ROPE_DIMS // 2)
    first_head = lane < SWA_HD
    qi = lax.broadcasted_iota(jnp.int32, (w, 2 * w), 0)
    kj = lax.broadcasted_iota(jnp.int32, (w, 2 * w), 1)
    dist = qi + w - kj
    band = (dist >= 0) & (dist < w)
    group = SWA_HEADS // SWA_KV_HEADS
    batch = 4

    kprev = kvprev_ref[:, :LANES]
    vprev = kvprev_ref[:, LANES:]
    for sb in range(q_ref.shape[0] // w):
        rs = slice(sb * w, (sb + 1) * w)
        cos = cos_ref[rs, :]
        sin = sin_ref[rs, :]

        def rot(x):
            swapped = jnp.where(low_half, pltpu.roll(x, LANES - ROPE_DIMS // 2, 1), pltpu.roll(x, ROPE_DIMS // 2, 1))
            return x * cos + swapped * sin

        kcur = rot(kv_ref[rs, :LANES])
        vcur = kv_ref[rs, LANES:]
        kk = jnp.concatenate([kprev, kcur], axis=0)
        vv = jnp.concatenate([vprev, vcur], axis=0)
        kk_sw = pltpu.roll(kk, SWA_HD, 1)
        vv_sw = pltpu.roll(vv, SWA_HD, 1)
        kdup = [jnp.where(first_head, kk, kk_sw).astype(bf16), jnp.where(first_head, kk_sw, kk).astype(bf16)]
        vlo = [jnp.where(first_head, vv, 0.0).astype(bf16), jnp.where(first_head, vv_sw, 0.0).astype(bf16)]
        vhi = [jnp.where(first_head, 0.0, vv_sw).astype(bf16), jnp.where(first_head, 0.0, vv).astype(bf16)]
        valid = band if sb > 0 else band & ((kj >= w) | (blk > 0))

        for t0 in range(0, SWA_HEADS // 2, batch):
            tiles = range(t0, t0 + batch)
            heads = [2 * t + half for t in tiles for half in range(2)]
            qts = [rot(q_ref[rs, t * LANES:(t + 1) * LANES]) * (SWA_HD ** -0.5) for t in tiles]
            qhs = [jnp.where(first_head if h % 2 == 0 else ~first_head, qts[h // 2 - t0], 0.0).astype(bf16)
                   for h in heads]
            ss = [jnp.where(valid, _dot_nt(qh, kdup[h // group]), -jnp.inf) for qh, h in zip(qhs, heads)]
            mxs = [jnp.maximum(jnp.max(s, axis=-1, keepdims=True), sink_ref[h]) for s, h in zip(ss, heads)]
            ps = [jnp.exp(s - mx) for s, mx in zip(ss, mxs)]
            dens = [jnp.sum(p, axis=-1, keepdims=True) + jnp.exp(sink_ref[h] - mx)
                    for p, mx, h in zip(ps, mxs, heads)]
            os_ = [_dot(p.astype(bf16), (vlo if h % 2 == 0 else vhi)[h // group]) / den
                   for p, den, h in zip(ps, dens, heads)]
            for t in tiles:
                k = 2 * (t - t0)
                o_ref[rs, t * LANES:(t + 1) * LANES] = (os_[k] + os_[k + 1]).astype(o_ref.dtype)
        kprev, vprev = kcur, vcur

    kvprev_ref[:, :LANES] = kprev
    kvprev_ref[:, LANES:] = vprev


def _swa(proj, cos_t, sin_t, sinks, bsz, seq, windows_per_step=4):
    m = proj.shape[0]
    tb = windows_per_step * WINDOW
    nb = seq // tb
    qw = SWA_HEADS * SWA_HD
    kvw = 2 * SWA_KV_HEADS * SWA_HD
    row = lambda b, i: b * nb + i
    return pl.pallas_call(
        _swa_kernel,
        grid=(bsz, nb),
        in_specs=[
            pl.BlockSpec(memory_space=pltpu.SMEM),
            pl.BlockSpec((tb, qw), lambda b, i: (row(b, i), COL_BQ // qw)),
            pl.BlockSpec((tb, kvw), lambda b, i: (row(b, i), COL_BKV // kvw)),
            pl.BlockSpec((tb, LANES), lambda b, i: (row(b, i), 0)),
            pl.BlockSpec((tb, LANES), lambda b, i: (row(b, i), 0)),
        ],
        out_specs=pl.BlockSpec((tb, qw), lambda b, i: (row(b, i), 0)),
        out_shape=jax.ShapeDtypeStruct((m, qw), bf16),
        scratch_shapes=[pltpu.VMEM((WINDOW, kvw), f32)],
        compiler_params=_cp("parallel", "arbitrary"),
        name="swa",
    )(sinks, proj, proj, cos_t, sin_t)


_GLA_LEVELS = (32, 16, 8, 4, 2, 1)


def _gla_constants():
    c = CHUNK
    t = np.arange(c)
    tri = (t[:, None] >= t[None, :]).astype(np.float32)
    sel = np.zeros((len(_GLA_LEVELS) * c, c), np.float32)
    for l, mhalf in enumerate(_GLA_LEVELS):
        mid = (t // (2 * mhalf)) * 2 * mhalf + mhalf - 1
        sel[l * c + t, mid] = 1.0
    cum_sel = np.concatenate([tri, sel @ tri], axis=0)
    n = GLA_HEADS * c
    idx = np.arange(n)
    head, tt = idx // c, idx % c
    same = head[:, None] == head[None, :]
    level = np.full((n, n), -1, np.int32)
    for l, mhalf in enumerate(_GLA_LEVELS):
        blk = (tt[:, None] // (2 * mhalf)) == (tt[None, :] // (2 * mhalf))
        msk = same & blk & ((tt[:, None] % (2 * mhalf)) >= mhalf) & ((tt[None, :] % (2 * mhalf)) < mhalf)
        level[msk] = l
    level[same & (tt[:, None] == tt[None, :])] = len(_GLA_LEVELS)
    return jnp.asarray(cum_sel, dtype=bf16), jnp.asarray(level)


def _stack_heads(x, width, heads):
    return jnp.concatenate([x[:, h * width:(h + 1) * width] for h in range(heads)], axis=0)


def _split_hi_lo(x):
    hi = x.astype(bf16)
    return hi, (x - hi.astype(f32)).astype(bf16)


def _gla_kernel(q_ref, k_ref, v_ref, g_ref, sm_ref, wgk_ref, bgk_ref, gain_ref, cumsel_ref, lv_ref, o_ref, st_ref):
    c = CHUNK

    @pl.when(pl.program_id(1) == 0)
    def _():
        st_ref[...] = jnp.zeros_like(st_ref)

    lane = lax.broadcasted_iota(jnp.int32, (1, GLA_HEADS * GLA_DK), 1)
    head_mask = [(lane // GLA_DK) == h for h in range(GLA_HEADS)]

    def per_head_rows(x):
        x16 = x.astype(bf16)
        return jnp.concatenate([jnp.where(head_mask[h], x16, jnp.zeros_like(x16)) for h in range(GLA_HEADS)], axis=0)

    log_a_all = jax.nn.log_sigmoid(_dot(sm_ref[...].astype(bf16), wgk_ref[...]) + bgk_ref[...]) * (1.0 / GLA_NORMALIZER)
    cum_sel = cumsel_ref[...]
    lv = lv_ref[...]
    subs = range(q_ref.shape[0] // c)
    rows = [slice(i * c, (i + 1) * c) for i in subs]
    pair = 2 * c
    qs = [q_ref[r, :] * (GLA_DK ** -0.5) for r in rows]
    ks = [k_ref[r, :] for r in rows]
    bbs = []
    for r in rows:
        la_hi, la_lo = _split_hi_lo(log_a_all[r, :])
        bbs.append(_dot(cum_sel, la_hi) + _dot(cum_sel, la_lo))
    bs = [bb[:c, :] for bb in bbs]
    accs = [[jnp.zeros((c, pair), f32) for _ in range(GLA_HEADS)] for _ in subs]
    for l in range(len(_GLA_LEVELS) + 1):
        for i in subs:
            q, k, b = qs[i], ks[i], bs[i]
            if l < len(_GLA_LEVELS):
                bm = bbs[i][(l + 1) * c:(l + 2) * c, :]
                qe = q * jnp.exp(jnp.minimum(b - bm, 0.0))
                ke = k * jnp.exp(jnp.minimum(bm - b, 0.0))
            else:
                qe, ke = q, k
            ke16 = ke.astype(bf16)
            res = _dot_nt(per_head_rows(qe), jnp.concatenate([ke16] * GLA_HEADS, axis=0))
            for h in range(GLA_HEADS):
                hr = slice(h * c, (h + 1) * c)
                hc = slice((h // 2) * pair, (h // 2 + 1) * pair)
                accs[i][h] = jnp.where(lv[hr, hc] == l, res[hr, hc], accs[i][h])
    vsts = [_stack_heads(v_ref[r, :], GLA_DV, GLA_HEADS).astype(bf16) for r in rows]
    o_intra = []
    for i in subs:
        parts = []
        for hp in range(GLA_HEADS // 2):
            a_pair = jnp.concatenate([accs[i][2 * hp], accs[i][2 * hp + 1]], axis=0).astype(bf16)
            parts.append(_dot(a_pair, vsts[i][hp * pair:(hp + 1) * pair, :]))
        o_intra.append(jnp.concatenate(parts, axis=0))
    qgs = [per_head_rows(q * jnp.exp(b)) for q, b in zip(qs, bs)]
    kds = [per_head_rows(k * jnp.exp(b[c - 1:c, :] - b)) for k, b in zip(ks, bs)]

    for i in subs:
        r, vst = rows[i], vsts[i]
        st = st_ref[...]
        o = o_intra[i] + _dot_nt(qgs[i], st.astype(bf16))
        st_ref[...] = st * jnp.exp(bs[i][c - 1:c, :]) + _dot_tn(vst, kds[i])
        ms = jnp.mean(o * o, axis=-1, keepdims=True)
        y = (o * lax.rsqrt(ms + EPS)) * gain_ref[...]
        y = y * _silu(_stack_heads(g_ref[r, :], GLA_DV, GLA_HEADS))
        for h in range(GLA_HEADS):
            o_ref[r, h * GLA_DV:(h + 1) * GLA_DV] = y[h * c:(h + 1) * c, :].astype(o_ref.dtype)


def _gla(proj, w_gk, b_gk, gain, bsz, seq, tb=1024):
    m = proj.shape[0]
    nb = seq // tb
    qk_w = GLA_HEADS * GLA_DK
    v_w = GLA_HEADS * GLA_DV
    cum_sel, level = _gla_constants()
    wgk_pad = jnp.zeros((LANES, qk_w), f32).at[SM_ALR:SM_ALR + GLA_RANK].set(w_gk).astype(bf16)
    row = lambda b, i: b * nb + i
    const = lambda shape: pl.BlockSpec(shape, lambda b, i: (0,) * len(shape))
    return pl.pallas_call(
        _gla_kernel,
        grid=(bsz, nb),
        in_specs=[
            pl.BlockSpec((tb, qk_w), lambda b, i: (row(b, i), COL_AQ // qk_w)),
            pl.BlockSpec((tb, qk_w), lambda b, i: (row(b, i), COL_AK // qk_w)),
            pl.BlockSpec((tb, v_w), lambda b, i: (row(b, i), COL_AV // v_w)),
            pl.BlockSpec((tb, v_w), lambda b, i: (row(b, i), COL_AG // v_w)),
            pl.BlockSpec((tb, LANES), lambda b, i: (row(b, i), COL_SMALL // LANES)),
            const((LANES, qk_w)), const((1, qk_w)), const((1, GLA_DV)),
            const(cum_sel.shape), const(level.shape),
        ],
        out_specs=pl.BlockSpec((tb, v_w), lambda b, i: (row(b, i), 0)),
        out_shape=jax.ShapeDtypeStruct((m, v_w), bf16),
        scratch_shapes=[pltpu.VMEM((GLA_DV, qk_w), f32)],
        compiler_params=_cp("parallel", "arbitrary"),
        name="gla",
    )(proj, proj, proj, proj, proj, wgk_pad, b_gk.reshape(1, qk_w), gain.reshape(1, GLA_DV), cum_sel, level)


_GDN_GROUP = 2


def _gdn_constants():
    c = CHUNK
    n = _GDN_GROUP * c
    idx = np.arange(n)
    head, tt = idx // c, idx % c
    same = head[:, None] == head[None, :]
    t = np.arange(c)
    tri = (t[:, None] >= t[None, :]).astype(np.float32)
    kind = np.zeros((n, n), np.int32)
    kind[same & (tt[:, None] == tt[None, :])] = 1
    kind[same & (tt[:, None] > tt[None, :])] = 2
    return jnp.asarray(tri, dtype=bf16), jnp.asarray(kind)


def _gdn_kernel(alog_ref, dtb_ref, cq_ref, ck_ref, cv_ref, cz_ref, sm_ref, cw_ref, gain_ref, tri_ref, kind_ref,
                o_ref, carry_ref, qkv_ref, s_ref):
    c = CHUNK
    tb = cq_ref.shape[0]
    hw = GDN_HEADS * GDN_DK

    @pl.when(pl.program_id(1) == 0)
    def _():
        carry_ref[...] = jnp.zeros_like(carry_ref)
        s_ref[...] = jnp.zeros_like(s_ref)

    row8 = lax.broadcasted_iota(jnp.int32, (8, LANES), 0)
    for part, src in enumerate((cq_ref, ck_ref, cv_ref)):
        for h in range(GDN_HEADS):
            col = part * hw + h * LANES
            x = src[:, h * LANES:(h + 1) * LANES]
            prev = carry_ref[:, col:col + LANES]
            acc = x * cw_ref[CONV_WIDTH - 1:CONV_WIDTH, col:col + LANES]
            for shift in range(1, CONV_WIDTH):
                xs = pltpu.roll(x, shift, 0)
                top = jnp.where(row8 < shift, pltpu.roll(prev, shift, 0), xs[0:8, :])
                xs = jnp.concatenate([top, xs[8:, :]], axis=0)
                acc = acc + xs * cw_ref[CONV_WIDTH - 1 - shift:CONV_WIDTH - shift, col:col + LANES]
            carry_ref[:, col:col + LANES] = x[tb - 8:tb, :]
            y = _silu(acc)
            if part < 2:
                y = y * lax.rsqrt(jnp.sum(y * y, axis=-1, keepdims=True) + EPS)
            if part == 0:
                y = y * (GDN_DK ** -0.5)
            qkv_ref[:, col:col + LANES] = y

    sm_all = sm_ref[...]
    g_all = -jnp.exp(alog_ref[...]) * jax.nn.softplus(sm_all + dtb_ref[...])
    beta_all = jax.nn.sigmoid(sm_all)
    tri = tri_ref[...]
    kind = kind_ref[...]

    def lane_to_rows(x, lane0):
        return jnp.concatenate(
            [jnp.broadcast_to(x[:, lane0 + h:lane0 + h + 1], (c, LANES)) for h in range(_GDN_GROUP)], axis=0)

    subs = range(tb // c)
    groups = range(GDN_HEADS // _GDN_GROUP)
    gw = _GDN_GROUP * GDN_DK
    units = [(i, gp) for i in subs for gp in groups]
    rows = [slice(i * c, (i + 1) * c) for i in subs]

    def cumsum3(g_f):
        g_hi = g_f.astype(bf16)
        g_mid, g_lo = _split_hi_lo(g_f - g_hi.astype(f32))
        return _dot(tri, g_hi) + (_dot(tri, g_mid) + _dot(tri, g_lo))

    def pair(i, gp, part):
        lo = part * hw + gp * gw
        return _stack_heads(qkv_ref[rows[i], lo:lo + gw], GDN_DK, _GDN_GROUP)

    gc_lanes = [cumsum3(g_all[r, :]) for r in rows]
    gcs = [lane_to_rows(gc_lanes[i], SM_CA + gp * _GDN_GROUP) for i, gp in units]
    betas = [lane_to_rows(beta_all[rows[i], :], SM_CB + gp * _GDN_GROUP) for i, gp in units]
    decs = [jnp.exp(jnp.minimum(gc - gc.T[0:1, :], 0.0)) for gc in gcs]
    qsts = [pair(i, gp, 0) for i, gp in units]
    ksts = [pair(i, gp, 1) for i, gp in units]
    vsts = [pair(i, gp, 2) for i, gp in units]
    egcs = [jnp.exp(gc) for gc in gcs]
    kbs = [kst * beta for kst, beta in zip(ksts, betas)]
    k16s = [kst.astype(bf16) for kst in ksts]
    m_negs = [jnp.where(kind == 2, -(_dot_nt(kb.astype(bf16), k16) * dec), 0.0)
              for kb, k16, dec in zip(kbs, k16s, decs)]
    attns = [jnp.where(kind >= 1, _dot_nt(qst.astype(bf16), k16) * dec, 0.0).astype(bf16)
             for qst, k16, dec in zip(qsts, k16s, decs)]
    rhss = [jnp.concatenate([vst * beta, kb * egc], axis=1)
            for vst, beta, kb, egc in zip(vsts, betas, kbs, egcs)]
    m_his = [m.astype(bf16) for m in m_negs]
    m_los = [(m - mh.astype(f32)).astype(bf16) for m, mh in zip(m_negs, m_his)]
    ps = m_his
    tinvs = m_negs
    for _ in range(4):
        ps = [_dot(p, p).astype(bf16) for p in ps]
        tinvs = [tinv + _dot(tinv.astype(bf16), p) + p.astype(f32) for tinv, p in zip(tinvs, ps)]
    t16s = [tinv.astype(bf16) for tinv in tinvs]
    x0s = [(rhs + _dot(t16, rhs.astype(bf16))).astype(bf16) for rhs, t16 in zip(rhss, t16s)]
    resids = [(rhs - x0.astype(f32)) + (_dot(mh, x0) + _dot(ml, x0))
              for rhs, x0, mh, ml in zip(rhss, x0s, m_his, m_los)]
    xss = [x0.astype(f32) + (resid + _dot(t16, resid.astype(bf16))) for x0, resid, t16 in zip(x0s, resids, t16s)]
    qgs = [qst * egc for qst, egc in zip(qsts, egcs)]

    for u, (i, gp) in enumerate(units):
        r, gc, kst, xs, qg, attn = rows[i], gcs[u], ksts[u], xss[u], qgs[u], attns[u]
        heads = [gp * _GDN_GROUP + hl for hl in range(_GDN_GROUP)]
        vnew_parts, oq_parts = [], []
        for hl, h in enumerate(heads):
            hs = slice(hl * c, (hl + 1) * c)
            s16 = s_ref[h].astype(bf16)
            t = _dot(jnp.concatenate([xs[hs, GDN_DV:], qg[hs, :]], axis=0).astype(bf16), s16)
            vnew_parts.append(xs[hs, :GDN_DV] - t[:c])
            oq_parts.append(t[c:])
        vnew = jnp.concatenate(vnew_parts, axis=0)
        v16 = vnew.astype(bf16)
        o = jnp.concatenate(oq_parts, axis=0) + _dot(attn, v16)
        for hl, h in enumerate(heads):
            hs = slice(hl * c, (hl + 1) * c)
            g_last = gc[(hl + 1) * c - 1:(hl + 1) * c, :]
            kd = (kst[hs, :] * jnp.exp(g_last - gc[hs, :])).astype(bf16)
            s_ref[h] = s_ref[h] * jnp.exp(g_last) + _dot_tn(kd, v16[hs, :])
        ms = jnp.mean(o * o, axis=-1, keepdims=True)
        y = (o * lax.rsqrt(ms + EPS)) * gain_ref[...]
        y = y * _silu(_stack_heads(cz_ref[r, gp * gw:(gp + 1) * gw], GDN_DV, _GDN_GROUP))
        for hl, h in enumerate(heads):
            o_ref[r, h * GDN_DV:(h + 1) * GDN_DV] = y[hl * c:(hl + 1) * c, :].astype(o_ref.dtype)


def _gdn(proj, conv_w, a_log, dt_bias, gain, bsz, seq, tb=512):
    m = proj.shape[0]
    nb = seq // tb
    hw = GDN_HEADS * GDN_DK
    tri, kind = _gdn_constants()
    row = lambda b, i: b * nb + i
    const = lambda shape: pl.BlockSpec(shape, lambda b, i: (0,) * len(shape))
    a_log = jnp.zeros((1, LANES), f32).at[0, SM_CA:SM_CA + GDN_HEADS].set(a_log)
    dt_bias = jnp.zeros((1, LANES), f32).at[0, SM_CA:SM_CA + GDN_HEADS].set(dt_bias)
    return pl.pallas_call(
        _gdn_kernel,
        grid=(bsz, nb),
        in_specs=[
            const((1, LANES)), const((1, LANES)),
            pl.BlockSpec((tb, hw), lambda b, i: (row(b, i), COL_CQ // hw)),
            pl.BlockSpec((tb, hw), lambda b, i: (row(b, i), COL_CK // hw)),
            pl.BlockSpec((tb, hw), lambda b, i: (row(b, i), COL_CV // hw)),
            pl.BlockSpec((tb, hw), lambda b, i: (row(b, i), COL_CZ // hw)),
            pl.BlockSpec((tb, LANES), lambda b, i: (row(b, i), COL_SMALL // LANES)),
            const((CONV_WIDTH, 3 * hw)), const((1, GDN_DV)), const(tri.shape), const(kind.shape),
        ],
        out_specs=pl.BlockSpec((tb, hw), lambda b, i: (row(b, i), 0)),
        out_shape=jax.ShapeDtypeStruct((m, hw), bf16),
        scratch_shapes=[
            pltpu.VMEM((8, 3 * hw), f32),
            pltpu.VMEM((tb, 3 * hw), f32),
            pltpu.VMEM((GDN_HEADS, GDN_DK, GDN_DV), f32),
        ],
        compiler_params=_cp("parallel", "arbitrary"),
        name="gdn",
    )(a_log, dt_bias, proj, proj, proj, proj, proj, conv_w, gain.reshape(1, GDN_DV), tri, kind)


def _permute_w_in(w_in):
    depth, d, _ = w_in.shape
    pieces = [
        w_in[:, :, 1552:2576],
        w_in[:, :, 512:1536],
        w_in[:, :, 2832:4880],
        w_in[:, :, 0:512],
        w_in[:, :, 2576:2832],
        w_in[:, :, 1536:1552],
        w_in[:, :, 4880:4888],
    ]
    used = sum(p.shape[-1] for p in pieces)
    pieces = [p.astype(bf16) for p in pieces]
    pieces.append(jnp.zeros((depth, d, PROJ_WIDTH - used), bf16))
    return jnp.concatenate(pieces, axis=-1)


def kernel(x, c, positions, w_mod, b_mod, norm1_gain, norm2_gain, w_in, gla_w_gk, gla_b_gk, gla_norm_gain,
           swa_sinks, gdn_conv_w, gdn_a_log, gdn_dt_bias, gdn_norm_gain, w_out, ffn_w_gate, ffn_w_up, ffn_w_down,
           final_norm_gain):
    bsz, seq, d = x.shape
    depth = w_in.shape[0]
    mods = _modulation(c, w_mod, b_mod)
    cos_t, sin_t = _rope_tables(positions)
    w_in_p = _permute_w_in(w_in)
    w_out16 = w_out.astype(bf16)
    wg16, wu16, wd16 = ffn_w_gate.astype(bf16), ffn_w_up.astype(bf16), ffn_w_down.astype(bf16)
    x2 = x.reshape(bsz * seq, d)
    g1 = norm1_gain.reshape(depth, 1, d)
    g2 = norm2_gain.reshape(depth, 1, d)
    for l in range(depth):
        proj = _in_proj(x2, mods, g1, w_in_p, l, seq)
        o_a = _gla(proj, gla_w_gk[l], gla_b_gk[l], gla_norm_gain[l], bsz, seq)
        o_b = _swa(proj, cos_t, sin_t, swa_sinks[l], bsz, seq)
        o_c = _gdn(proj, gdn_conv_w[l], gdn_a_log[l], gdn_dt_bias[l], gdn_norm_gain[l], bsz, seq)
        x2 = _out_proj(x2, o_a, o_b, o_c, w_out16, mods, l, seq)
        x2 = _ffn(x2, mods, g2, final_norm_gain, wg16, wu16, wd16, l, seq, final_norm=(l == depth - 1))
    return x2.reshape(bsz, seq, d)
```
